```python
import numpy as np
import jax
import jax.numpy as jnp
from jax import lax


D_MODEL = 1024
BATCH = 4
SEQ = 4096
DEPTH = 2

HEAD_DIM = 64
N_HEADS = D_MODEL // HEAD_DIM
N_KV_HEADS = 4
GROUP = N_HEADS // N_KV_HEADS
ROPE_THETA = 10000.0
RMS_EPS = 1e-6
NEG_INF = -1e30
POS_INF = 1e30
IDX_HEADS = 8
IDX_DIM = HEAD_DIM
DSA_TOPK = 256
DSA_Q_BLOCK = 128
CMP_LEN = 32
CMP_STRIDE = 16
CMP_HIDDEN = 256
SLC_LEN = 64
SLC_TOPN = 16
WINDOW = 512
NSA_Q_BLOCK = 64
N_NSA_BRANCH = 3
D_FF = 256 * ((8 * D_MODEL // 3 + 255) // 256)
N_EXPERTS = 8
TOP_K_EXPERTS = 2
D_FF_EXPERT = 7 * D_MODEL // 2
N_A_LAYERS = max(1, DEPTH // 2)
N_B_LAYERS = DEPTH - N_A_LAYERS
N_DENSE_LAYERS = (DEPTH + 1) // 2
N_MOE_LAYERS = DEPTH // 2
A_WIDTHS = (N_HEADS * HEAD_DIM, N_KV_HEADS * HEAD_DIM, N_KV_HEADS * HEAD_DIM, IDX_HEADS * IDX_DIM, IDX_DIM, IDX_HEADS)
A_IN = sum(A_WIDTHS)
B_IN = N_HEADS * HEAD_DIM + N_NSA_BRANCH * N_HEADS
KV_WIDTH = 2 * N_NSA_BRANCH * N_KV_HEADS * HEAD_DIM

kernel_name = 'yoco_dsa_nsa_adaln_moe_trunk'

F32 = jnp.float32


def rmsnorm(x, g):
    xf = x.astype(F32)
    y = xf * lax.rsqrt(jnp.mean(xf * xf, axis=-1, keepdims=True) + RMS_EPS)
    return (y * g.astype(F32)).astype(x.dtype)


def modulate(u, shift, scale):
    return u * (1.0 + scale[:, None, :]) + shift[:, None, :]


def rope_tables(positions, dim):
    inv = 1.0 / (ROPE_THETA ** (jnp.arange(0, dim, 2, dtype=F32) / dim))
    ang = positions.astype(F32)[..., None] * inv
    return jnp.cos(ang), jnp.sin(ang)


def apply_rope(x, cos, sin):
    x1, x2 = jnp.split(x.astype(F32), 2, axis=-1)
    c = cos[:, :, None, :]
    s = sin[:, :, None, :]
    return jnp.concatenate([x1 * c - x2 * s, x2 * c + x1 * s], axis=-1).astype(x.dtype)


def masked_softmax(s, mask):
    p = jax.nn.softmax(jnp.where(mask, s, NEG_INF), axis=-1)
    return jnp.where(mask, p, 0.0)


def swiglu(u, wg, wu, wd):
    return (jax.nn.silu(u @ wg) * (u @ wu)) @ wd


def moe_swiglu(u, w_router, wg, wu, wd):
    bsz, seq, d = u.shape
    xf = u.reshape(bsz * seq, d)
    logits = (xf @ w_router).astype(F32)
    vals, idx = lax.top_k(logits, TOP_K_EXPERTS)
    w = jax.nn.softmax(vals, axis=-1)
    gate = jnp.sum(jax.nn.one_hot(idx, N_EXPERTS, dtype=F32) * w[..., None], axis=1)
    y = jnp.zeros_like(xf)
    for e in range(N_EXPERTS):
        y = y + gate[:, e:e + 1].astype(xf.dtype) * swiglu(xf, wg[e], wu[e], wd[e])
    return y.reshape(bsz, seq, d)


def dsa_mixer(u, cos, sin, w_in, w_out):
    bsz, seq, _ = u.shape
    splits = [int(i) for i in np.cumsum(A_WIDTHS[:-1])]
    q, k, v, iq, ik, iw = jnp.split(u @ w_in, splits, axis=-1)
    q = apply_rope(q.reshape(bsz, seq, N_HEADS, HEAD_DIM), cos, sin)
    k = apply_rope(k.reshape(bsz, seq, N_KV_HEADS, HEAD_DIM), cos, sin)
    v = v.reshape(bsz, seq, N_KV_HEADS, HEAD_DIM)
    iq = apply_rope(iq.reshape(bsz, seq, IDX_HEADS, IDX_DIM), cos, sin).astype(F32)
    ik = apply_rope(ik.reshape(bsz, seq, 1, IDX_DIM), cos, sin)[:, :, 0].astype(F32)
    iw = iw.astype(F32) * (IDX_HEADS ** -0.5 * IDX_DIM ** -0.5)
    n_keep = min(DSA_TOPK, seq // 4)
    key_pos = jnp.arange(seq)
    scale = HEAD_DIM ** -0.5

    def block(i):
        q0 = i * DSA_Q_BLOCK
        t = q0 + jnp.arange(DSA_Q_BLOCK)
        qb = lax.dynamic_slice_in_dim(q, q0, DSA_Q_BLOCK, axis=1)
        iqb = lax.dynamic_slice_in_dim(iq, q0, DSA_Q_BLOCK, axis=1)
        iwb = lax.dynamic_slice_in_dim(iw, q0, DSA_Q_BLOCK, axis=1)
        rel = jax.nn.relu(jnp.einsum('bqhd,bsd->bqhs', iqb, ik))
        score = jnp.einsum('bqh,bqhs->bqs', iwb, rel)
        score = jnp.where(key_pos[None, None, :] <= t[None, :, None], score, NEG_INF)
        _, idx = lax.top_k(score, n_keep)
        kg = jax.vmap(lambda kb, ib: kb[ib])(k, idx)
        vg = jax.vmap(lambda vb, ib: vb[ib])(v, idx)
        qg = qb.reshape(bsz, DSA_Q_BLOCK, N_KV_HEADS, GROUP, HEAD_DIM)
        s = jnp.einsum('bqgrd,bqkgd->bgrqk', qg, kg).astype(F32) * scale
        valid = (idx <= t[None, :, None])[:, None, None]
        p = masked_softmax(s, valid).astype(vg.dtype)
        o = jnp.einsum('bgrqk,bqkgd->bqgrd', p, vg)
        return o.reshape(bsz, DSA_Q_BLOCK, N_HEADS * HEAD_DIM)

    o = lax.map(block, jnp.arange(seq // DSA_Q_BLOCK))
    o = jnp.transpose(o, (1, 0, 2, 3)).reshape(bsz, seq, N_HEADS * HEAD_DIM)
    return o @ w_out


def compress_blocks(x, blk_idx, pe, w1, w2):
    bsz = x.shape[0]
    n_cmp = blk_idx.shape[0]
    xb = x[:, blk_idx] + pe[None, None, :, None, :]
    xb = jnp.transpose(xb, (0, 1, 3, 2, 4)).reshape(bsz, n_cmp, N_KV_HEADS, CMP_LEN * HEAD_DIM)
    return jax.nn.gelu(xb @ w1) @ w2


def nsa_shared_kv(h, c_act, kv_gain, w_kv_ada, b_kv_ada, w_kv, pe_k, w1_k, w2_k, pe_v, w1_v, w2_v, cos, sin):
    bsz, seq, _ = h.shape
    shift, scale = jnp.split(c_act @ w_kv_ada + b_kv_ada, 2, axis=-1)
    u = modulate(rmsnorm(h, kv_gain), shift, scale)
    kv = (u @ w_kv).reshape(bsz, seq, 2 * N_NSA_BRANCH, N_KV_HEADS, HEAD_DIM)
    k_cmp = apply_rope(kv[:, :, 0], cos, sin)
    v_cmp = kv[:, :, 1]
    k_slc = apply_rope(kv[:, :, 2], cos, sin)
    v_slc = kv[:, :, 3]
    k_win = apply_rope(kv[:, :, 4], cos, sin)
    v_win = kv[:, :, 5]
    n_cmp = (seq - CMP_LEN) // CMP_STRIDE + 1
    blk_idx = np.arange(n_cmp)[:, None] * CMP_STRIDE + np.arange(CMP_LEN)[None, :]
    kc = compress_blocks(k_cmp, blk_idx, pe_k, w1_k, w2_k)
    vc = compress_blocks(v_cmp, blk_idx, pe_v, w1_v, w2_v)
    return kc, vc, k_slc, v_slc, k_win, v_win


def nsa_mixer(u, cos, sin, kc, vc, k_slc, v_slc, k_win, v_win, w_q, w_out):
    bsz, seq, _ = u.shape
    proj = u @ w_q
    q = apply_rope(proj[..., :N_HEADS * HEAD_DIM].reshape(bsz, seq, N_HEADS, HEAD_DIM), cos, sin)
    gates = jax.nn.sigmoid(proj[..., N_HEADS * HEAD_DIM:].astype(F32)).reshape(bsz, seq, N_HEADS, N_NSA_BRANCH)
    n_cmp = kc.shape[1]
    n_slc = seq // SLC_LEN
    n_sel = min(SLC_TOPN, n_slc)
    cmp_start = np.arange(n_cmp) * CMP_STRIDE
    cmp_end = jnp.asarray(cmp_start + CMP_LEN - 1)
    slc_start = np.arange(n_slc) * SLC_LEN
    ov = np.minimum(cmp_start[:, None] + CMP_LEN, slc_start[None, :] + SLC_LEN) - np.maximum(cmp_start[:, None], slc_start[None, :])
    agg = jnp.asarray(np.clip(ov, 0, None) / CMP_LEN, dtype=F32)
    ks_blk = jnp.transpose(k_slc.reshape(bsz, n_slc, SLC_LEN, N_KV_HEADS, HEAD_DIM), (0, 3, 1, 2, 4))
    vs_blk = jnp.transpose(v_slc.reshape(bsz, n_slc, SLC_LEN, N_KV_HEADS, HEAD_DIM), (0, 3, 1, 2, 4))
    kw_pad = jnp.pad(k_win, ((0, 0), (WINDOW, 0), (0, 0), (0, 0)))
    vw_pad = jnp.pad(v_win, ((0, 0), (WINDOW, 0), (0, 0), (0, 0)))
    blk = jnp.arange(n_slc)
    in_blk = jnp.arange(SLC_LEN)
    scale = HEAD_DIM ** -0.5

    def block(i):
        q0 = i * NSA_Q_BLOCK
        t = q0 + jnp.arange(NSA_Q_BLOCK)
        qb = lax.dynamic_slice_in_dim(q, q0, NSA_Q_BLOCK, axis=1)
        qb = jnp.transpose(qb.reshape(bsz, NSA_Q_BLOCK, N_KV_HEADS, GROUP, HEAD_DIM), (0, 2, 3, 1, 4))
        gb = lax.dynamic_slice_in_dim(gates, q0, NSA_Q_BLOCK, axis=1)
        gb = jnp.transpose(gb.reshape(bsz, NSA_Q_BLOCK, N_KV_HEADS, GROUP, N_NSA_BRANCH), (0, 2, 3, 1, 4))
        sc = jnp.einsum('bgrqd,bngd->bgrqn', qb, kc).astype(F32) * scale
        pc = masked_softmax(sc, cmp_end[None, :] <= t[:, None])
        oc = jnp.einsum('bgrqn,bngd->bgrqd', pc.astype(vc.dtype), vc)
        imp = jnp.einsum('bgrqn,nj->bgqj', pc, agg)
        jt = t // SLC_LEN
        valid_b = blk[None, :] * SLC_LEN <= t[:, None]
        forced = (blk[None, :] == 0) | (blk[None, :] == jt[:, None]) | (blk[None, :] == jt[:, None] - 1)
        imp = jnp.where(valid_b, jnp.where(forced, POS_INF, imp), NEG_INF)
        _, sel = lax.top_k(imp, n_sel)
        kg = jax.vmap(jax.vmap(lambda kb, sb: kb[sb]))(ks_blk, sel)
        vg = jax.vmap(jax.vmap(lambda vb, sb: vb[sb]))(vs_blk, sel)
        tok = sel[..., None] * SLC_LEN + in_blk
        ss = jnp.einsum('bgrqd,bgqnld->bgrqnl', qb, kg).astype(F32) * scale
        ss = ss.reshape(bsz, N_KV_HEADS, GROUP, NSA_Q_BLOCK, n_sel * SLC_LEN)
        ms = (tok <= t[None, None, :, None, None]).reshape(bsz, N_KV_HEADS, 1, NSA_Q_BLOCK, n_sel * SLC_LEN)
        ps = masked_softmax(ss, ms)
        os_ = jnp.einsum('bgrqm,bgqmd->bgrqd', ps.astype(vg.dtype), vg.reshape(bsz, N_KV_HEADS, NSA_Q_BLOCK, n_sel * SLC_LEN, HEAD_DIM))
        kwb = lax.dynamic_slice_in_dim(kw_pad, q0, NSA_Q_BLOCK + WINDOW, axis=1)
        vwb = lax.dynamic_slice_in_dim(vw_pad, q0, NSA_Q_BLOCK + WINDOW, axis=1)
        s_pos = q0 - WINDOW + jnp.arange(NSA_Q_BLOCK + WINDOW)
        mw = (s_pos[None, :] >= 0) & (s_pos[None, :] <= t[:, None]) & (s_pos[None, :] > t[:, None] - WINDOW)
        sw = jnp.einsum('bgrqd,bkgd->bgrqk', qb, kwb).astype(F32) * scale
        pw = masked_softmax(sw, mw)
        ow = jnp.einsum('bgrqk,bkgd->bgrqd', pw.astype(vwb.dtype), vwb)
        o = gb[..., 0:1] * oc.astype(F32) + gb[..., 1:2] * os_.astype(F32) + gb[..., 2:3] * ow.astype(F32)
        return jnp.transpose(o, (0, 3, 1, 2, 4)).reshape(bsz, NSA_Q_BLOCK, N_HEADS * HEAD_DIM).astype(u.dtype)

    o = lax.map(block, jnp.arange(seq // NSA_Q_BLOCK))
    o = jnp.transpose(o, (1, 0, 2, 3)).reshape(bsz, seq, N_HEADS * HEAD_DIM)
    return o @ w_out


def setup_inputs(seed: int = 0) -> dict:
    key = jax.random.key(seed)
    ks = jax.random.split(key, 32)
    d = D_MODEL

    def nrm(k, shape, scale):
        return jax.random.normal(k, shape, F32) * scale

    offset = jax.random.randint(ks[2], (BATCH, 1), 0, 1024, dtype=jnp.int32)
    positions = (offset + jnp.arange(SEQ, dtype=jnp.int32)[None, :]).astype(jnp.int32)
    return {
        'x': nrm(ks[0], (BATCH, SEQ, d), 1.0),
        'c': nrm(ks[1], (BATCH, d), 1.0),
        'positions': positions,
        'attn_gain': 1.0 + nrm(ks[3], (DEPTH, d), 0.05),
        'ffn_gain': 1.0 + nrm(ks[4], (DEPTH, d), 0.05),
        'w_ada': nrm(ks[5], (DEPTH, d, 6 * d), 0.5 * d ** -0.5),
        'b_ada': nrm(ks[6], (DEPTH, 6 * d), 0.01),
        'a_w_in': nrm(ks[7], (N_A_LAYERS, d, A_IN), d ** -0.5),
        'a_w_out': nrm(ks[8], (N_A_LAYERS, N_HEADS * HEAD_DIM, d), (N_HEADS * HEAD_DIM) ** -0.5),
        'b_w_q': nrm(ks[9], (N_B_LAYERS, d, B_IN), d ** -0.5),
        'b_w_out': nrm(ks[10], (N_B_LAYERS, N_HEADS * HEAD_DIM, d), (N_HEADS * HEAD_DIM) ** -0.5),
        'kv_gain': 1.0 + nrm(ks[11], (d,), 0.05),
        'w_kv_ada': nrm(ks[12], (d, 2 * d), 0.5 * d ** -0.5),
        'b_kv_ada': nrm(ks[13], (2 * d,), 0.01),
        'w_kv': nrm(ks[14], (d, KV_WIDTH), d ** -0.5),
        'cmp_pe_k': nrm(ks[15], (CMP_LEN, HEAD_DIM), 0.5),
        'cmp_w1_k': nrm(ks[16], (CMP_LEN * HEAD_DIM, CMP_HIDDEN), (CMP_LEN * HEAD_DIM) ** -0.5),
        'cmp_w2_k': nrm(ks[17], (CMP_HIDDEN, HEAD_DIM), CMP_HIDDEN ** -0.5),
        'cmp_pe_v': nrm(ks[18], (CMP_LEN, HEAD_DIM), 0.5),
        'cmp_w1_v': nrm(ks[19], (CMP_LEN * HEAD_DIM, CMP_HIDDEN), (CMP_LEN * HEAD_DIM) ** -0.5),
        'cmp_w2_v': nrm(ks[20], (CMP_HIDDEN, HEAD_DIM), CMP_HIDDEN ** -0.5),
        'ffn_w_gate': nrm(ks[21], (N_DENSE_LAYERS, d, D_FF), d ** -0.5),
        'ffn_w_up': nrm(ks[22], (N_DENSE_LAYERS, d, D_FF), d ** -0.5),
        'ffn_w_down': nrm(ks[23], (N_DENSE_LAYERS, D_FF, d), D_FF ** -0.5),
        'moe_w_router': nrm(ks[24], (N_MOE_LAYERS, d, N_EXPERTS), d ** -0.5),
        'moe_w_gate': nrm(ks[25], (N_MOE_LAYERS, N_EXPERTS, d, D_FF_EXPERT), d ** -0.5),
        'moe_w_up': nrm(ks[26], (N_MOE_LAYERS, N_EXPERTS, d, D_FF_EXPERT), d ** -0.5),
        'moe_w_down': nrm(ks[27], (N_MOE_LAYERS, N_EXPERTS, D_FF_EXPERT, d), D_FF_EXPERT ** -0.5),
        'final_gain': 1.0 + nrm(ks[28], (d,), 0.05),
    }


def reference(x, c, positions, attn_gain, ffn_gain, w_ada, b_ada, a_w_in, a_w_out, b_w_q, b_w_out,
              kv_gain, w_kv_ada, b_kv_ada, w_kv, cmp_pe_k, cmp_w1_k, cmp_w2_k, cmp_pe_v, cmp_w1_v, cmp_w2_v,
              ffn_w_gate, ffn_w_up, ffn_w_down, moe_w_router, moe_w_gate, moe_w_up, moe_w_down, final_gain):
    cos, sin = rope_tables(positions, HEAD_DIM)
    c_act = jax.nn.silu(c)
    h = x
    kc = vc = k_slc = v_slc = k_win = v_win = None
    for layer in range(DEPTH):
        mod = c_act @ w_ada[layer] + b_ada[layer]
        a_shift, a_scale, a_gate, f_shift, f_scale, f_gate = jnp.split(mod, 6, axis=-1)
        if layer == N_A_LAYERS:
            kc, vc, k_slc, v_slc, k_win, v_win = nsa_shared_kv(
                h, c_act, kv_gain, w_kv_ada, b_kv_ada, w_kv,
                cmp_pe_k, cmp_w1_k, cmp_w2_k, cmp_pe_v, cmp_w1_v, cmp_w2_v, cos, sin)
        u = modulate(rmsnorm(h, attn_gain[layer]), a_shift, a_scale)
        if layer < N_A_LAYERS:
            mix = dsa_mixer(u, cos, sin, a_w_in[layer], a_w_out[layer])
        else:
            j = layer - N_A_LAYERS
            mix = nsa_mixer(u, cos, sin, kc, vc, k_slc, v_slc, k_win, v_win, b_w_q[j], b_w_out[j])
        h = h + a_gate[:, None, :] * mix
        u = modulate(rmsnorm(h, ffn_gain[layer]), f_shift, f_scale)
        if layer % 2 == 0:
            ff = swiglu(u, ffn_w_gate[layer // 2], ffn_w_up[layer // 2], ffn_w_down[layer // 2])
        else:
            m = layer // 2
            ff = moe_swiglu(u, moe_w_router[m], moe_w_gate[m], moe_w_up[m], moe_w_down[m])
        h = h + f_gate[:, None, :] * ff
    return rmsnorm(h, final_gain)
```

```python
import functools

import numpy as np
import jax
import jax.numpy as jnp
from jax import lax
from jax.experimental import pallas as pl
from jax.experimental.pallas import tpu as pltpu

F32 = jnp.float32
BF16 = jnp.bfloat16

D_MODEL = 1024
HEAD_DIM = 64
N_HEADS = 16
N_KV_HEADS = 4
GROUP = N_HEADS // N_KV_HEADS
ROPE_THETA = 10000.0
RMS_EPS = 1e-6
NEG_INF = -1e30
POS_INF = 1e30
IDX_HEADS = 8
IDX_DIM = HEAD_DIM
DSA_TOPK = 256
CMP_LEN = 32
CMP_STRIDE = 16
CMP_HIDDEN = 256
SLC_LEN = 64
SLC_SHIFT = 6
SLC_TOPN = 16
WINDOW = 512
N_NSA_BRANCH = 3
N_EXPERTS = 8

LANE = 128
TQ = 128
TK = 512
WIN_SPAN = WINDOW + TQ
INT_MIN = np.int32(-2 ** 31)
M_INIT = -3.0e38
VMEM_LIMIT = 56 * 1024 * 1024

_NT = (((1,), (1,)), ((), ()))


def _cparams(*sem):
    return pltpu.CompilerParams(dimension_semantics=sem, vmem_limit_bytes=VMEM_LIMIT)


def _rope_kernel(pos_ref, inv_ref, sgn_ref, cos_ref, sin_ref):
    ang = pos_ref[0].astype(F32) * inv_ref[...]
    cos_ref[0] = jnp.cos(ang)
    sin_ref[0] = jnp.sin(ang) * sgn_ref[...]


def _rope_tables(positions):
    b, s = positions.shape
    inv = 1.0 / (ROPE_THETA ** (jnp.arange(0, HEAD_DIM, 2, dtype=F32) / HEAD_DIM))
    inv = jnp.tile(inv, LANE // (HEAD_DIM // 2))[None]
    sgn = np.where((np.arange(LANE) % HEAD_DIM) < HEAD_DIM // 2, -1.0, 1.0).astype(np.float32)[None]
    return pl.pallas_call(
        _rope_kernel,
        grid=(b,),
        in_specs=[pl.BlockSpec((1, s, 1), lambda i: (i, 0, 0)),
                  pl.BlockSpec((1, LANE), lambda i: (0, 0)),
                  pl.BlockSpec((1, LANE), lambda i: (0, 0))],
        out_specs=[pl.BlockSpec((1, s, LANE), lambda i: (i, 0, 0))] * 2,
        out_shape=[jax.ShapeDtypeStruct((b, s, LANE), F32)] * 2,
        compiler_params=_cparams("arbitrary"),
        name="rope_tables",
    )(positions[..., None], inv, jnp.asarray(sgn))


def _ada_kernel(c_ref, w_ref, b_ref, o_ref):
    c = c_ref[...]
    ca = c * jax.nn.sigmoid(c)
    o_ref[0] = jnp.dot(ca, w_ref[0], preferred_element_type=F32,
                       precision=lax.Precision.HIGHEST) + b_ref[0]


def _ada(c_pad, w, bias):
    nl, d, n = w.shape
    tn = 1024
    return pl.pallas_call(
        _ada_kernel,
        grid=(nl, n // tn),
        in_specs=[pl.BlockSpec((8, d), lambda l, j: (0, 0)),
                  pl.BlockSpec((1, d, tn), lambda l, j: (l, 0, j)),
                  pl.BlockSpec((1, 1, tn), lambda l, j: (l, 0, j))],
        out_specs=pl.BlockSpec((1, 8, tn), lambda l, j: (l, 0, j)),
        out_shape=jax.ShapeDtypeStruct((nl, 8, n), F32),
        compiler_params=_cparams("arbitrary", "arbitrary"),
        name="ada_mod",
    )(c_pad, w, bias.reshape(nl, 1, n))


def _norm_mod(x, g, shift, scale):
    ms = jnp.mean(x * x, axis=-1, keepdims=True)
    u = (x * lax.rsqrt(ms + RMS_EPS)) * g
    return u * (1.0 + scale) + shift


def _proj_kernel(h_ref, g_ref, sh_ref, sc_ref, cos_ref, sin_ref, w_ref, *out_refs, plan):
    ub = _norm_mod(h_ref[0], g_ref[...], sh_ref[0], sc_ref[0]).astype(BF16)
    cos = cos_ref[0]
    sin = sin_ref[0]
    lane = lax.broadcasted_iota(jnp.int32, cos.shape, 1)
    first = (lane & (HEAD_DIM - 1)) < HEAD_DIM // 2
    for (ws, width, rope, scale, oi, os_, act) in plan:
        for c0 in range(0, width, 256):
            cw = min(256, width - c0)
            y = jnp.dot(ub, w_ref[:, ws + c0: ws + c0 + cw], preferred_element_type=F32)
            for j in range(cw // LANE):
                blk = y[:, j * LANE:(j + 1) * LANE]
                if rope:
                    sw = jnp.where(first, pltpu.roll(blk, LANE - HEAD_DIM // 2, 1),
                                   pltpu.roll(blk, HEAD_DIM // 2, 1))
                    blk = blk * cos + sw * sin
                if scale != 1.0:
                    blk = blk * scale
                if act == "sigmoid":
                    blk = jax.nn.sigmoid(blk)
                o0 = os_ + c0 + j * LANE
                out_refs[oi][0, :, o0:o0 + LANE] = blk.astype(out_refs[oi].dtype)


def _norm_proj(h, gain, shift, scale, cos_t, sin_t, w, plan, outs, tm=512):
    b, s, d = h.shape
    n = w.shape[1]
    vec = pl.BlockSpec((1, 1, d), lambda bi, i: (bi, 0, 0))
    return pl.pallas_call(
        functools.partial(_proj_kernel, plan=tuple(plan)),
        grid=(b, s // tm),
        in_specs=[pl.BlockSpec((1, tm, d), lambda bi, i: (bi, i, 0)),
                  pl.BlockSpec((1, d), lambda bi, i: (0, 0)),
                  vec, vec,
                  pl.BlockSpec((1, tm, LANE), lambda bi, i: (bi, i, 0)),
                  pl.BlockSpec((1, tm, LANE), lambda bi, i: (bi, i, 0)),
                  pl.BlockSpec((d, n), lambda bi, i: (0, 0))],
        out_specs=[pl.BlockSpec((1, tm, wd), lambda bi, i: (bi, i, 0)) for wd, _ in outs],
        out_shape=[jax.ShapeDtypeStruct((b, s, wd), dt) for wd, dt in outs],
        compiler_params=_cparams("arbitrary", "arbitrary"),
        name="norm_proj",
    )(h, gain.reshape(1, d), shift, scale, cos_t, sin_t, w)


def _indexer_kernel(iq_ref, ik_ref, iw_ref, tri_ref, bias_ref, key_scr, *, n_keep):
    i = pl.program_id(1)
    q0 = i * TQ
    s_len = bias_ref.shape[2]
    n_all = s_len // TK
    nch = (q0 + TQ + TK - 1) // TK
    iw = iw_ref[0]
    t = q0 + lax.broadcasted_iota(jnp.int32, (TQ, 1), 0)

    def score_chunk(c, carry):
        k0 = pl.multiple_of(c * TK, TK)
        kk = ik_ref[0, pl.ds(k0, TK), :]
        acc = jnp.zeros((TQ, TK), F32)
        for hh in range(IDX_HEADS):
            r = lax.dot_general(iq_ref[0, :, hh * LANE:(hh + 1) * LANE], kk, _NT,
                                preferred_element_type=F32)
            acc = acc + iw[:, hh:hh + 1] * jnp.maximum(r, 0.0)
        bits = lax.bitcast_convert_type(acc, jnp.int32)
        key = bits ^ ((bits >> 31) & np.int32(0x7FFFFFFF))
        pos = k0 + lax.broadcasted_iota(jnp.int32, (1, TK), 1)
        key_scr[:, pl.ds(k0, TK)] = jnp.where(pos <= t, key, INT_MIN)
        return carry

    lax.fori_loop(0, nch, score_chunk, 0)

    def count_ge(cand):
        def body(c, part):
            kc = key_scr[:, pl.ds(pl.multiple_of(c * TK, TK), TK)]
            ge = (kc >= cand).astype(jnp.int32)
            for j in range(TK // LANE):
                part = part + ge[:, j * LANE:(j + 1) * LANE]
            return part
        part = lax.fori_loop(0, nch, body, jnp.zeros((TQ, LANE), jnp.int32))
        return jnp.sum(part, axis=1, keepdims=True)

    thr_u = jnp.zeros((TQ, 1), jnp.int32)
    cnt_thr = jnp.zeros((TQ, 1), jnp.int32) + nch * TK
    for bit in range(31, -1, -1):
        cand_u = thr_u | (INT_MIN if bit == 31 else np.int32(1 << bit))
        cnt = count_ge(cand_u ^ INT_MIN)
        ok = cnt >= n_keep
        thr_u = jnp.where(ok, cand_u, thr_u)
        cnt_thr = jnp.where(ok, cnt, cnt_thr)
    thr = thr_u ^ INT_MIN
    has_thr = thr > INT_MIN
    thr_eff = jnp.maximum(thr, INT_MIN + 1)

    def write_bias(c, sel):
        bias_ref[0, :, pl.ds(pl.multiple_of(c * TK, TK), TK)] = jnp.where(sel, 0.0, NEG_INF).astype(BF16)

    tied = jnp.max(jnp.where(has_thr & (cnt_thr > n_keep), 1, 0)) > 0

    @pl.when(jnp.logical_not(tied))
    def _():
        def body(c, carry):
            kc = key_scr[:, pl.ds(pl.multiple_of(c * TK, TK), TK)]
            write_bias(c, kc >= thr_eff)
            return carry
        lax.fori_loop(0, nch, body, 0)

    @pl.when(tied)
    def _():
        need = (n_keep - count_ge(thr_eff + 1)).astype(F32)

        def body(c, seen):
            kc = key_scr[:, pl.ds(pl.multiple_of(c * TK, TK), TK)]
            tie = kc == thr_eff
            tie_b = jnp.where(tie, 1.0, 0.0).astype(BF16)
            pref = jnp.dot(tie_b, tri_ref[...], preferred_element_type=F32) + seen
            take = (pref <= need) | jnp.logical_not(has_thr)
            write_bias(c, (kc > thr_eff) | (tie & take))
            return pref[:, TK - 1:TK]
        lax.fori_loop(0, nch, body, jnp.zeros((TQ, 1), F32))

    def fill(c, carry):
        bias_ref[0, :, pl.ds(pl.multiple_of(c * TK, TK), TK)] = jnp.full((TQ, TK), NEG_INF, BF16)
        return carry
    lax.fori_loop(nch, n_all, fill, 0)


def _dsa_bias(proj, iw, s_len):
    b = proj.shape[0]
    n_keep = min(DSA_TOPK, s_len // 4)
    tri = jnp.asarray(np.triu(np.ones((TK, TK), np.float32)), BF16)
    return pl.pallas_call(
        functools.partial(_indexer_kernel, n_keep=n_keep),
        grid=(b, s_len // TQ),
        in_specs=[pl.BlockSpec((1, TQ, IDX_HEADS * LANE), lambda bi, i: (bi, i, 2)),
                  pl.BlockSpec((1, s_len, LANE), lambda bi, i: (bi, 0, 30)),
                  pl.BlockSpec((1, TQ, LANE), lambda bi, i: (bi, i, 0)),
                  pl.BlockSpec((TK, TK), lambda bi, i: (0, 0))],
        out_specs=pl.BlockSpec((1, TQ, s_len), lambda bi, i: (bi, i, 0)),
        out_shape=jax.ShapeDtypeStruct((b, s_len, s_len), BF16),
        scratch_shapes=[pltpu.VMEM((TQ, s_len), jnp.int32)],
        compiler_params=_cparams("arbitrary", "arbitrary"),
        name="dsa_indexer",
    )(proj, proj, iw, tri)


def _stack_heads(q_ref, qs_scr):
    for g in range(N_KV_HEADS):
        for r in range(GROUP):
            hh = g * GROUP + r
            qs_scr[g, r * TQ:(r + 1) * TQ, :] = q_ref[0, :, hh * LANE:(hh + 1) * LANE]


def _online_update(s, v, m_scr, l_scr, acc_scr, g):
    m_old = m_scr[g]
    m_new = jnp.maximum(m_old, jnp.max(s, axis=1, keepdims=True))
    alpha = jnp.exp(m_old - m_new)
    p = jnp.exp(s - m_new)
    l_scr[g] = alpha * l_scr[g] + jnp.sum(p, axis=1, keepdims=True)
    acc_scr[g] = alpha * acc_scr[g] + jnp.dot(p.astype(BF16), v, preferred_element_type=F32)
    m_scr[g] = m_new


def _init_online(m_scr, l_scr, acc_scr):
    m_scr[...] = jnp.full(m_scr.shape, M_INIT, F32)
    l_scr[...] = jnp.zeros(l_scr.shape, F32)
    acc_scr[...] = jnp.zeros(acc_scr.shape, F32)


def _vhalf(g):
    return slice((g // 2) * LANE, (g // 2 + 1) * LANE)


def _dsa_attn_kernel(q_ref, k_ref, v_ref, bias_ref, o_ref, qs_scr, m_scr, l_scr, acc_scr):
    i = pl.program_id(1)
    nch = (i * TQ + TQ + TK - 1) // TK
    _stack_heads(q_ref, qs_scr)
    _init_online(m_scr, l_scr, acc_scr)

    def chunk(c, carry):
        k0 = pl.multiple_of(c * TK, TK)
        bias = bias_ref[0, :, pl.ds(k0, TK)].astype(F32)
        bias4 = jnp.concatenate([bias] * GROUP, axis=0)
        for g in range(N_KV_HEADS):
            kg = k_ref[0, pl.ds(k0, TK), g * LANE:(g + 1) * LANE]
            s = lax.dot_general(qs_scr[g], kg, _NT, preferred_element_type=F32) + bias4
            _online_update(s, v_ref[0, pl.ds(k0, TK), _vhalf(g)], m_scr, l_scr, acc_scr, g)
        return carry

    lax.fori_loop(0, nch, chunk, 0)
    for g in range(N_KV_HEADS):
        og = acc_scr[g] / l_scr[g]
        for r in range(GROUP):
            hh = g * GROUP + r
            o_ref[0, :, hh * LANE:(hh + 1) * LANE] = og[r * TQ:(r + 1) * TQ].astype(BF16)


def _dsa_attention(proj, bias):
    b, s_len, _ = proj.shape
    return pl.pallas_call(
        _dsa_attn_kernel,
        grid=(b, s_len // TQ),
        in_specs=[pl.BlockSpec((1, TQ, N_HEADS * LANE), lambda bi, i: (bi, i, 0)),
                  pl.BlockSpec((1, s_len, N_KV_HEADS * LANE), lambda bi, i: (bi, 0, 6)),
                  pl.BlockSpec((1, s_len, N_KV_HEADS * HEAD_DIM), lambda bi, i: (bi, 0, 14)),
                  pl.BlockSpec((1, TQ, s_len), lambda bi, i: (bi, i, 0))],
        out_specs=pl.BlockSpec((1, TQ, N_HEADS * LANE), lambda bi, i: (bi, i, 0)),
        out_shape=jax.ShapeDtypeStruct((b, s_len, N_HEADS * LANE), BF16),
        scratch_shapes=[pltpu.VMEM((N_KV_HEADS, GROUP * TQ, LANE), BF16),
                        pltpu.VMEM((N_KV_HEADS, GROUP * TQ, 1), F32),
                        pltpu.VMEM((N_KV_HEADS, GROUP * TQ, 1), F32),
                        pltpu.VMEM((N_KV_HEADS, GROUP * TQ, LANE), F32)],
        compiler_params=_cparams("arbitrary", "arbitrary"),
        name="dsa_attention",
    )(proj, proj, proj, bias)


def _outproj_kernel(o_ref, w_ref, gate_ref, h_ref, out_ref):
    y = jnp.dot(o_ref[0], w_ref[...], preferred_element_type=F32)
    out_ref[0] = h_ref[0] + gate_ref[0] * y


def _outproj_residual(o, w, gate, h, tm=512):
    b, s, d = h.shape
    k = o.shape[2]
    return pl.pallas_call(
        _outproj_kernel,
        grid=(b, s // tm),
        in_specs=[pl.BlockSpec((1, tm, k), lambda bi, i: (bi, i, 0)),
                  pl.BlockSpec((k, d), lambda bi, i: (0, 0)),
                  pl.BlockSpec((1, 1, d), lambda bi, i: (bi, 0, 0)),
                  pl.BlockSpec((1, tm, d), lambda bi, i: (bi, i, 0))],
        out_specs=pl.BlockSpec((1, tm, d), lambda bi, i: (bi, i, 0)),
        out_shape=jax.ShapeDtypeStruct((b, s, d), F32),
        compiler_params=_cparams("arbitrary", "arbitrary"),
        name="outproj_residual",
    )(o, w, gate, h)


def _ffn_kernel(h_ref, g_ref, sh_ref, sc_ref, gate_ref, wg_ref, wu_ref, wd_ref, out_ref, u_scr, acc_scr):
    f = pl.program_id(2)

    @pl.when(f == 0)
    def _():
        u_scr[...] = _norm_mod(h_ref[0], g_ref[...], sh_ref[0], sc_ref[0]).astype(BF16)
        acc_scr[...] = jnp.zeros(acc_scr.shape, F32)

    ub = u_scr[...]
    a = jnp.dot(ub, wg_ref[...], preferred_element_type=F32)
    up = jnp.dot(ub, wu_ref[...], preferred_element_type=F32)
    hid = (a * jax.nn.sigmoid(a)) * up
    acc_scr[...] += jnp.dot(hid.astype(BF16), wd_ref[...], preferred_element_type=F32)

    @pl.when(f == pl.num_programs(2) - 1)
    def _():
        out_ref[0] = h_ref[0] + gate_ref[0] * acc_scr[...]


def _ffn_residual(h, gain, shift, scale, gate, wg, wu, wd, tm=512, tf=1408):
    b, s, d = h.shape
    dff = wg.shape[1]
    vec = pl.BlockSpec((1, 1, d), lambda bi, i, f: (bi, 0, 0))
    return pl.pallas_call(
        _ffn_kernel,
        grid=(b, s // tm, dff // tf),
        in_specs=[pl.BlockSpec((1, tm, d), lambda bi, i, f: (bi, i, 0)),
                  pl.BlockSpec((1, d), lambda bi, i, f: (0, 0)),
                  vec, vec, vec,
                  pl.BlockSpec((d, tf), lambda bi, i, f: (0, f)),
                  pl.BlockSpec((d, tf), lambda bi, i, f: (0, f)),
                  pl.BlockSpec((tf, d), lambda bi, i, f: (f, 0))],
        out_specs=pl.BlockSpec((1, tm, d), lambda bi, i, f: (bi, i, 0)),
        out_shape=jax.ShapeDtypeStruct((b, s, d), F32),
        scratch_shapes=[pltpu.VMEM((tm, d), BF16), pltpu.VMEM((tm, d), F32)],
        compiler_params=_cparams("arbitrary", "arbitrary", "arbitrary"),
        name="ffn_residual",
    )(h, gain.reshape(1, d), shift, scale, gate, wg, wu, wd)


def _compress_kernel(x_ref, pe_ref, w1_ref, w2_ref, o_ref):
    x = x_ref[0, 0, 0]
    nr = x.shape[0]
    pa = jnp.dot((x + pe_ref[0, 0]).astype(BF16), w1_ref[0, 0], preferred_element_type=F32)
    pb = jnp.dot((x + pe_ref[0, 1]).astype(BF16), w1_ref[0, 1], preferred_element_type=F32)
    hid = pa + pltpu.roll(pb, nr - 1, 0)
    act = 0.5 * hid * (1.0 + jnp.tanh(np.float32(np.sqrt(2.0 / np.pi)) * (hid + 0.044715 * (hid * hid * hid))))
    o_ref[0, 0, 0] = jnp.dot(act.astype(BF16), w2_ref[0], preferred_element_type=F32).astype(BF16)


def _compress(x, pe, w1, w2):
    two, b, g, nr, kk = x.shape
    return pl.pallas_call(
        _compress_kernel,
        grid=(two, b, g),
        in_specs=[pl.BlockSpec((1, 1, 1, nr, kk), lambda a, bi, gi: (a, bi, gi, 0, 0)),
                  pl.BlockSpec((1, 2, 1, kk), lambda a, bi, gi: (a, 0, 0, 0)),
                  pl.BlockSpec((1, 2, kk, CMP_HIDDEN), lambda a, bi, gi: (a, 0, 0, 0)),
                  pl.BlockSpec((1, CMP_HIDDEN, HEAD_DIM), lambda a, bi, gi: (a, 0, 0))],
        out_specs=pl.BlockSpec((1, 1, 1, nr, HEAD_DIM), lambda a, bi, gi: (a, bi, gi, 0, 0)),
        out_shape=jax.ShapeDtypeStruct((two, b, g, nr, HEAD_DIM), BF16),
        compiler_params=_cparams("arbitrary", "arbitrary", "arbitrary"),
        name="nsa_compress",
    )(x, pe, w1, w2)


def _nsa_kernel(q_ref, gt_ref, kc_ref, vc_ref, ks_ref, kw_ref, vs_ref, vw_ref, aggt_ref, o_ref,
                kaug_scr, qs_scr, oc_scr, m_scr, l_scr, acc_scr, *, n_sel):
    i = pl.program_id(1)
    q0 = i * TQ
    nch = (q0 + TQ + TK - 1) // TK

    @pl.when(i == 0)
    def _():
        for c in range(kaug_scr.shape[0] // TK):
            row = c * TK + lax.broadcasted_iota(jnp.int32, (TK, N_KV_HEADS * LANE), 0)
            lane = lax.broadcasted_iota(jnp.int32, (TK, N_KV_HEADS * LANE), 1) & (LANE - 1)
            onehot = jnp.where(lane - HEAD_DIM == (row >> SLC_SHIFT), 1.0, 0.0).astype(BF16)
            kaug_scr[c * TK:(c + 1) * TK, :] = ks_ref[0, c * TK:(c + 1) * TK, :] + onehot

    _stack_heads(q_ref, qs_scr)
    _init_online(m_scr, l_scr, acc_scr)
    t = q0 + lax.broadcasted_iota(jnp.int32, (TQ, 1), 0)
    t4 = jnp.concatenate([t] * GROUP, axis=0)

    n_cmp = kc_ref.shape[1]
    cmp_end = lax.broadcasted_iota(jnp.int32, (1, n_cmp), 1) * CMP_STRIDE + (CMP_LEN - 1)
    cmask = cmp_end <= t4
    n_blk = aggt_ref.shape[0]
    jrow = lax.broadcasted_iota(jnp.int32, (n_blk, TQ), 0)
    tq = q0 + lax.broadcasted_iota(jnp.int32, (n_blk, TQ), 1)
    jt = tq >> SLC_SHIFT
    valid = jrow * SLC_LEN <= tq
    forced = (jrow == 0) | (jrow == jt) | (jrow == jt - 1)
    for g in range(N_KV_HEADS):
        sc = lax.dot_general(qs_scr[g], kc_ref[0, :, g * LANE:(g + 1) * LANE], _NT, preferred_element_type=F32)
        sm = jnp.where(cmask, sc, NEG_INF)
        e = jnp.where(cmask, jnp.exp(sm - jnp.max(sm, axis=1, keepdims=True)), 0.0)
        pc = e / jnp.maximum(jnp.sum(e, axis=1, keepdims=True), 1e-30)
        oc_scr[g] = jnp.dot(pc.astype(BF16), vc_ref[0, :, _vhalf(g)], preferred_element_type=F32)
        pcs = pc[0:TQ]
        for r in range(1, GROUP):
            pcs = pcs + pc[r * TQ:(r + 1) * TQ]
        imp = lax.dot_general(aggt_ref[...], pcs, _NT, preferred_element_type=F32,
                              precision=lax.Precision.HIGHEST)
        imp = jnp.where(valid, jnp.where(forced, POS_INF, imp), NEG_INF)
        rank = jnp.zeros((n_blk, TQ), jnp.int32)
        for j in range(n_blk):
            row = imp[j:j + 1, :]
            beats = (row > imp) | ((row == imp) & (jrow > j))
            rank = rank + beats.astype(jnp.int32)
        sel = (rank < n_sel) & valid
        selb = jnp.where(sel, 0.0, NEG_INF)
        pads = [jnp.zeros((HEAD_DIM, TQ), F32), selb]
        if n_blk < LANE - HEAD_DIM:
            pads.append(jnp.zeros((LANE - HEAD_DIM - n_blk, TQ), F32))
        selb = jnp.concatenate(pads, axis=0)
        selb = selb.T.astype(BF16)
        qs_scr[g] = qs_scr[g] + jnp.concatenate([selb] * GROUP, axis=0)

    def slc_chunk(c, diagonal):
        k0 = pl.multiple_of(c * TK, TK)
        for g in range(N_KV_HEADS):
            s = lax.dot_general(qs_scr[g], kaug_scr[pl.ds(k0, TK), g * LANE:(g + 1) * LANE], _NT,
                                preferred_element_type=F32)
            if diagonal:
                pos = k0 + lax.broadcasted_iota(jnp.int32, (1, TK), 1)
                s = jnp.where(pos <= t4, s, NEG_INF)
            _online_update(s, vs_ref[0, pl.ds(k0, TK), _vhalf(g)], m_scr, l_scr, acc_scr, g)

    def slc_body(c, carry):
        slc_chunk(c, False)
        return carry

    lax.fori_loop(0, nch - 1, slc_body, 0)
    slc_chunk(nch - 1, True)

    gt = gt_ref[0]
    w0 = pl.multiple_of(jnp.maximum(q0 - WINDOW, 0), TQ)
    wpos = w0 + lax.broadcasted_iota(jnp.int32, (1, WIN_SPAN), 1)
    wmask = (wpos <= t4) & (wpos > t4 - WINDOW)
    for g in range(N_KV_HEADS):
        s = lax.dot_general(qs_scr[g], kw_ref[0, pl.ds(w0, WIN_SPAN), g * LANE:(g + 1) * LANE], _NT,
                            preferred_element_type=F32)
        s = jnp.where(wmask, s, NEG_INF)
        p = jnp.exp(s - jnp.max(s, axis=1, keepdims=True))
        p = jnp.where(wmask, p, 0.0)
        ow = jnp.dot(p.astype(BF16), vw_ref[0, pl.ds(w0, WIN_SPAN), _vhalf(g)], preferred_element_type=F32)
        ow = ow / jnp.sum(p, axis=1, keepdims=True)
        osl = acc_scr[g] / l_scr[g]
        oc = oc_scr[g]
        for r in range(GROUP):
            hh = g * GROUP + r
            rs = slice(r * TQ, (r + 1) * TQ)
            o = (gt[:, hh:hh + 1] * oc[rs] + gt[:, N_HEADS + hh:N_HEADS + hh + 1] * osl[rs]
                 + gt[:, 2 * N_HEADS + hh:2 * N_HEADS + hh + 1] * ow[rs])
            o_ref[0, :, hh * LANE:(hh + 1) * LANE] = o.astype(BF16)


def _nsa_attention(q, gates, kc, vc, kv, agg_t):
    b, s_len, _ = q.shape
    n_cmp = kc.shape[1]
    n_blk = agg_t.shape[0]
    n_sel = min(SLC_TOPN, s_len // SLC_LEN)
    rows = GROUP * TQ
    return pl.pallas_call(
        functools.partial(_nsa_kernel, n_sel=n_sel),
        grid=(b, s_len // TQ),
        in_specs=[pl.BlockSpec((1, TQ, N_HEADS * LANE), lambda bi, i: (bi, i, 0)),
                  pl.BlockSpec((1, TQ, LANE), lambda bi, i: (bi, i, 0)),
                  pl.BlockSpec((1, n_cmp, N_KV_HEADS * LANE), lambda bi, i: (bi, 0, 0)),
                  pl.BlockSpec((1, n_cmp, N_KV_HEADS * HEAD_DIM), lambda bi, i: (bi, 0, 0)),
                  pl.BlockSpec((1, s_len, N_KV_HEADS * LANE), lambda bi, i: (bi, 0, 0)),
                  pl.BlockSpec((1, s_len, N_KV_HEADS * LANE), lambda bi, i: (bi, 0, 1)),
                  pl.BlockSpec((1, s_len, N_KV_HEADS * HEAD_DIM), lambda bi, i: (bi, 0, 4)),
                  pl.BlockSpec((1, s_len, N_KV_HEADS * HEAD_DIM), lambda bi, i: (bi, 0, 5)),
                  pl.BlockSpec((n_blk, n_cmp), lambda bi, i: (0, 0))],
        out_specs=pl.BlockSpec((1, TQ, N_HEADS * LANE), lambda bi, i: (bi, i, 0)),
        out_shape=jax.ShapeDtypeStruct((b, s_len, N_HEADS * LANE), BF16),
        scratch_shapes=[pltpu.VMEM((s_len, N_KV_HEADS * LANE), BF16),
                        pltpu.VMEM((N_KV_HEADS, rows, LANE), BF16),
                        pltpu.VMEM((N_KV_HEADS, rows, LANE), F32),
                        pltpu.VMEM((N_KV_HEADS, rows, 1), F32),
                        pltpu.VMEM((N_KV_HEADS, rows, 1), F32),
                        pltpu.VMEM((N_KV_HEADS, rows, LANE), F32)],
        compiler_params=_cparams("arbitrary", "arbitrary"),
        name="nsa_attention",
    )(q, gates, kc, vc, kv, kv, kv, kv, agg_t)


def _moe_kernel(h_ref, g_ref, sh_ref, sc_ref, gate_ref, wr_ref, fg_ref, wg_ref, wu_ref, wd_ref, out_ref,
                u_scr, rw_scr, acc_scr):
    e = pl.program_id(2)
    f = pl.program_id(3)

    @pl.when((e == 0) & (f == 0))
    def _():
        u = _norm_mod(h_ref[0], g_ref[...], sh_ref[0], sc_ref[0])
        u_scr[...] = u.astype(BF16)
        acc_scr[...] = jnp.zeros(acc_scr.shape, F32)
        logits = jnp.dot(u, wr_ref[...], preferred_element_type=F32, precision=lax.Precision.HIGHEST)
        lane = lax.broadcasted_iota(jnp.int32, logits.shape, 1)
        logits = jnp.where(lane < N_EXPERTS, logits, -jnp.inf)
        m1 = jnp.max(logits, axis=1, keepdims=True)
        i1 = jnp.min(jnp.where(logits == m1, lane, LANE), axis=1, keepdims=True)
        rest = jnp.where(lane == i1, -jnp.inf, logits)
        m2 = jnp.max(rest, axis=1, keepdims=True)
        i2 = jnp.min(jnp.where(rest == m2, lane, LANE), axis=1, keepdims=True)
        e2 = jnp.exp(m2 - m1)
        w1 = 1.0 / (1.0 + e2)
        w2 = e2 / (1.0 + e2)
        rw_scr[...] = jnp.where(lane == i1, w1, 0.0) + jnp.where(lane == i2, w2, 0.0)

    ub = u_scr[...]
    a = jnp.dot(ub, wg_ref[0], preferred_element_type=F32)
    up = jnp.dot(ub, wu_ref[0], preferred_element_type=F32)
    hid = (a * jax.nn.sigmoid(a)) * up
    y = jnp.dot(hid.astype(BF16), wd_ref[0], preferred_element_type=F32)
    rw = rw_scr[...]
    lane = lax.broadcasted_iota(jnp.int32, rw.shape, 1)
    we = jnp.sum(jnp.where(lane == e, rw, 0.0), axis=1, keepdims=True)
    acc_scr[...] += we * y

    @pl.when((e == pl.num_programs(2) - 1) & (f == pl.num_programs(3) - 1))
    def _():
        hn = h_ref[0] + gate_ref[0] * acc_scr[...]
        ms = jnp.mean(hn * hn, axis=-1, keepdims=True)
        out_ref[0] = (hn * lax.rsqrt(ms + RMS_EPS)) * fg_ref[...]


def _moe_residual_norm(h, gain, shift, scale, gate, w_router, final_gain, wg, wu, wd, tm=512, tf=512):
    b, s, d = h.shape
    ne, _, dff = wg.shape
    vec = pl.BlockSpec((1, 1, d), lambda bi, i, e, f: (bi, 0, 0))
    row = pl.BlockSpec((1, d), lambda bi, i, e, f: (0, 0))
    return pl.pallas_call(
        _moe_kernel,
        grid=(b, s // tm, ne, dff // tf),
        in_specs=[pl.BlockSpec((1, tm, d), lambda bi, i, e, f: (bi, i, 0)),
                  row, vec, vec, vec,
                  pl.BlockSpec((d, LANE), lambda bi, i, e, f: (0, 0)),
                  row,
                  pl.BlockSpec((1, d, tf), lambda bi, i, e, f: (e, 0, f)),
                  pl.BlockSpec((1, d, tf), lambda bi, i, e, f: (e, 0, f)),
                  pl.BlockSpec((1, tf, d), lambda bi, i, e, f: (e, f, 0))],
        out_specs=pl.BlockSpec((1, tm, d), lambda bi, i, e, f: (bi, i, 0)),
        out_shape=jax.ShapeDtypeStruct((b, s, d), F32),
        scratch_shapes=[pltpu.VMEM((tm, d), BF16), pltpu.VMEM((tm, LANE), F32), pltpu.VMEM((tm, d), F32)],
        compiler_params=_cparams("arbitrary", "arbitrary", "arbitrary", "arbitrary"),
        name="moe_residual_norm",
    )(h, gain.reshape(1, d), shift, scale, gate, w_router, final_gain.reshape(1, d), wg, wu, wd)


def _pad_heads(w, n_heads):
    d = w.shape[0]
    w = w.reshape(d, n_heads, HEAD_DIM)
    return jnp.pad(w, ((0, 0), (0, 0), (0, LANE - HEAD_DIM))).reshape(d, n_heads * LANE)


def _pad_cols(w, n):
    return jnp.pad(w, ((0, 0), (0, n - w.shape[1])))


def _out_rows(w_out):
    d = w_out.shape[1]
    w = w_out.reshape(N_KV_HEADS, GROUP, HEAD_DIM, d)
    z = jnp.zeros_like(w)
    even = jnp.concatenate([w, z], axis=2)
    odd = jnp.concatenate([z, w], axis=2)
    sel = (np.arange(N_KV_HEADS) % 2 == 0)[:, None, None, None]
    return jnp.where(sel, even, odd).reshape(N_HEADS * LANE, d)


def _dsa_weights(w_in):
    q, k, v, iq, ik, iw = jnp.split(w_in, [1024, 1280, 1536, 2048, 2112], axis=1)
    w = jnp.concatenate([_pad_heads(q, N_HEADS), _pad_heads(iq, IDX_HEADS), _pad_heads(k, N_KV_HEADS), v,
                         _pad_cols(ik, LANE), _pad_cols(iw, LANE)], axis=1).astype(BF16)
    plan = [(0, 2048, True, HEAD_DIM ** -0.5, 0, 0, None),
            (2048, 1024, True, 1.0, 0, 2048, None),
            (3072, 512, True, 1.0, 0, 3072, None),
            (3584, 256, False, 1.0, 0, 3584, None),
            (3840, 128, True, 1.0, 0, 3840, None),
            (3968, 128, False, IDX_HEADS ** -0.5 * IDX_DIM ** -0.5, 1, 0, None)]
    return w, plan, [(3968, BF16), (LANE, F32)]


def _nsa_q_weights(w_q):
    q = w_q[:, :N_HEADS * HEAD_DIM]
    gates = w_q[:, N_HEADS * HEAD_DIM:].reshape(-1, N_HEADS, N_NSA_BRANCH)
    gates = jnp.transpose(gates, (0, 2, 1)).reshape(-1, N_NSA_BRANCH * N_HEADS)
    w = jnp.concatenate([_pad_heads(q, N_HEADS), _pad_cols(gates, LANE)], axis=1).astype(BF16)
    plan = [(0, 2048, True, HEAD_DIM ** -0.5, 0, 0, None),
            (2048, 128, False, 1.0, 1, 0, "sigmoid")]
    return w, plan, [(2048, BF16), (LANE, F32)]


def _kv_weights(w_kv):
    kvw = w_kv.reshape(-1, 2 * N_NSA_BRANCH, N_KV_HEADS * HEAD_DIM)
    k_cmp, v_cmp, k_slc, v_slc, k_win, v_win = [kvw[:, j] for j in range(2 * N_NSA_BRANCH)]
    w = jnp.concatenate([_pad_heads(k_slc, N_KV_HEADS), _pad_heads(k_win, N_KV_HEADS), v_slc, v_win,
                         k_cmp, v_cmp], axis=1).astype(BF16)
    plan = [(0, 512, True, 1.0, 0, 0, None),
            (512, 512, True, 1.0, 0, 512, None),
            (1024, 256, False, 1.0, 0, 1024, None),
            (1280, 256, False, 1.0, 0, 1280, None),
            (1536, 256, True, 1.0, 1, 0, None),
            (1792, 256, False, 1.0, 1, 256, None)]
    return w, plan, [(1536, BF16), (512, F32)]


def _nsa_constants(s_len):
    n_cmp = (s_len - CMP_LEN) // CMP_STRIDE + 1
    n_slc = s_len // SLC_LEN
    cmp_start = np.arange(n_cmp) * CMP_STRIDE
    slc_start = np.arange(n_slc) * SLC_LEN
    ov = (np.minimum(cmp_start[:, None] + CMP_LEN, slc_start[None, :] + SLC_LEN)
          - np.maximum(cmp_start[:, None], slc_start[None, :]))
    agg = (np.clip(ov, 0, None) / CMP_LEN).astype(np.float32)
    agg_t = np.zeros((n_slc, s_len // CMP_STRIDE), np.float32)
    agg_t[:, :n_cmp] = agg.T
    return jnp.asarray(agg_t)


def kernel(x, c, positions, attn_gain, ffn_gain, w_ada, b_ada, a_w_in, a_w_out, b_w_q, b_w_out, kv_gain, w_kv_ada, b_kv_ada, w_kv, cmp_pe_k, cmp_w1_k, cmp_w2_k, cmp_pe_v, cmp_w1_v, cmp_w2_v, ffn_w_gate, ffn_w_up, ffn_w_down, moe_w_router, moe_w_gate, moe_w_up, moe_w_down, final_gain):
    b, s_len, d = x.shape
    assert s_len % TK == 0 and s_len // SLC_LEN <= HEAD_DIM and s_len >= WIN_SPAN and b <= 8

    cos_t, sin_t = _rope_tables(positions)
    c_pad = jnp.pad(c, ((0, 8 - b), (0, 0)))
    mod = _ada(c_pad, w_ada, b_ada)[:, :b]
    kv_mod = _ada(c_pad, w_kv_ada[None], b_kv_ada[None])[0, :b]

    def parts(m, n):
        return [p[:, None, :] for p in jnp.split(m, n, axis=-1)]

    a_shift, a_scale, a_gate, f_shift, f_scale, f_gate = parts(mod[0], 6)
    w, plan, outs = _dsa_weights(a_w_in[0])
    proj, iw = _norm_proj(x, attn_gain[0], a_shift, a_scale, cos_t, sin_t, w, plan, outs)
    bias = _dsa_bias(proj, iw, s_len)
    o = _dsa_attention(proj, bias)
    h = _outproj_residual(o, _out_rows(a_w_out[0]).astype(BF16), a_gate, x)
    h = _ffn_residual(h, ffn_gain[0], f_shift, f_scale, f_gate,
                      ffn_w_gate[0].astype(BF16), ffn_w_up[0].astype(BF16), ffn_w_down[0].astype(BF16))

    kv_shift, kv_scale = parts(kv_mod, 2)
    w, plan, outs = _kv_weights(w_kv)
    kv, kv_cmp = _norm_proj(h, kv_gain, kv_shift, kv_scale, cos_t, sin_t, w, plan, outs)
    nr = s_len // CMP_STRIDE
    xc = kv_cmp.reshape(b, s_len, 2, N_KV_HEADS, HEAD_DIM)
    xc = jnp.transpose(xc, (2, 0, 3, 1, 4)).reshape(2, b, N_KV_HEADS, nr, CMP_STRIDE * HEAD_DIM)
    pe = jnp.stack([cmp_pe_k, cmp_pe_v]).reshape(2, 2, 1, CMP_STRIDE * HEAD_DIM)
    w1 = jnp.stack([cmp_w1_k, cmp_w1_v]).reshape(2, 2, CMP_STRIDE * HEAD_DIM, CMP_HIDDEN).astype(BF16)
    w2 = jnp.stack([cmp_w2_k, cmp_w2_v]).astype(BF16)
    cmp = _compress(xc, pe, w1, w2)
    cmp = jnp.transpose(cmp, (0, 1, 3, 2, 4))
    kc = jnp.pad(cmp[0], ((0, 0), (0, 0), (0, 0), (0, LANE - HEAD_DIM))).reshape(b, nr, N_KV_HEADS * LANE)
    vc = cmp[1].reshape(b, nr, N_KV_HEADS * HEAD_DIM)

    a_shift, a_scale, a_gate, f_shift, f_scale, f_gate = parts(mod[1], 6)
    w, plan, outs = _nsa_q_weights(b_w_q[0])
    q, gates = _norm_proj(h, attn_gain[1], a_shift, a_scale, cos_t, sin_t, w, plan, outs)
    o = _nsa_attention(q, gates, kc, vc, kv, _nsa_constants(s_len))
    h = _outproj_residual(o, _out_rows(b_w_out[0]).astype(BF16), a_gate, h)
    return _moe_residual_norm(h, ffn_gain[1], f_shift, f_scale, f_gate, _pad_cols(moe_w_router[0], LANE),
                              final_gain, moe_w_gate[0].astype(BF16), moe_w_up[0].astype(BF16),
                              moe_w_down[0].astype(BF16))
```

```python
import functools

import numpy as np
import jax
import jax.numpy as jnp
from jax import lax
from jax.experimental import pallas as pl
from jax.experimental.pallas import tpu as pltpu

F32 = jnp.float32
BF16 = jnp.bfloat16

D_MODEL = 1024
HEAD_DIM = 64
N_HEADS = 16
N_KV_HEADS = 4
GROUP = N_HEADS // N_KV_HEADS
ROPE_THETA = 10000.0
RMS_EPS = 1e-6
NEG_INF = -1e30
POS_INF = 1e30
IDX_HEADS = 8
IDX_DIM = HEAD_DIM
DSA_TOPK = 256
CMP_LEN = 32
CMP_STRIDE = 16
CMP_HIDDEN = 256
SLC_LEN = 64
SLC_SHIFT = 6
SLC_TOPN = 16
WINDOW = 512
N_NSA_BRANCH = 3
N_EXPERTS = 8

LANE = 128
TQ = 128
TQI = 256
RB = 64
TK = 512
WIN_SPAN = WINDOW + TQ
INT_MIN = np.int32(-2 ** 31)
M_INIT = -3.0e38
VMEM_LIMIT = 56 * 1024 * 1024

_NT = (((1,), (1,)), ((), ()))


def _cparams(*sem):
    return pltpu.CompilerParams(dimension_semantics=sem, vmem_limit_bytes=VMEM_LIMIT)


def _rope_kernel(pos_ref, inv_ref, sgn_ref, cos_ref, sin_ref):
    ang = pos_ref[0].astype(F32) * inv_ref[...]
    cos_ref[0] = jnp.cos(ang)
    sin_ref[0] = jnp.sin(ang) * sgn_ref[...]


def _rope_tables(positions):
    b, s = positions.shape
    inv = 1.0 / (ROPE_THETA ** (jnp.arange(0, HEAD_DIM, 2, dtype=F32) / HEAD_DIM))
    inv = jnp.tile(inv, LANE // (HEAD_DIM // 2))[None]
    sgn = np.where((np.arange(LANE) % HEAD_DIM) < HEAD_DIM // 2, -1.0, 1.0).astype(np.float32)[None]
    return pl.pallas_call(
        _rope_kernel,
        grid=(b,),
        in_specs=[pl.BlockSpec((1, s, 1), lambda i: (i, 0, 0)),
                  pl.BlockSpec((1, LANE), lambda i: (0, 0)),
                  pl.BlockSpec((1, LANE), lambda i: (0, 0))],
        out_specs=[pl.BlockSpec((1, s, LANE), lambda i: (i, 0, 0))] * 2,
        out_shape=[jax.ShapeDtypeStruct((b, s, LANE), F32)] * 2,
        compiler_params=_cparams("arbitrary"),
        name="rope_tables",
    )(positions[..., None], inv, jnp.asarray(sgn))


def _ada_kernel(c_ref, w_ref, b_ref, o_ref):
    c = c_ref[...]
    ca = c * jax.nn.sigmoid(c)
    o_ref[0] = jnp.dot(ca, w_ref[0], preferred_element_type=F32,
                       precision=lax.Precision.HIGHEST) + b_ref[0]


def _ada(c_pad, w, bias):
    nl, d, n = w.shape
    tn = 1024
    return pl.pallas_call(
        _ada_kernel,
        grid=(nl, n // tn),
        in_specs=[pl.BlockSpec((8, d), lambda l, j: (0, 0)),
                  pl.BlockSpec((1, d, tn), lambda l, j: (l, 0, j)),
                  pl.BlockSpec((1, 1, tn), lambda l, j: (l, 0, j))],
        out_specs=pl.BlockSpec((1, 8, tn), lambda l, j: (l, 0, j)),
        out_shape=jax.ShapeDtypeStruct((nl, 8, n), F32),
        compiler_params=_cparams("arbitrary", "arbitrary"),
        name="ada_mod",
    )(c_pad, w, bias.reshape(nl, 1, n))


def _norm_mod(x, g, shift, scale):
    ms = jnp.mean(x * x, axis=-1, keepdims=True)
    u = (x * lax.rsqrt(ms + RMS_EPS)) * g
    return u * (1.0 + scale) + shift


def _proj_kernel(h_ref, g_ref, sh_ref, sc_ref, cos_ref, sin_ref, w_ref, *out_refs, plan):
    ub = _norm_mod(h_ref[0], g_ref[...], sh_ref[0], sc_ref[0]).astype(BF16)
    cos = cos_ref[0]
    sin = sin_ref[0]
    lane = lax.broadcasted_iota(jnp.int32, cos.shape, 1)
    first = (lane & (HEAD_DIM - 1)) < HEAD_DIM // 2
    for (ws, width, rope, scale, oi, os_, act) in plan:
        for c0 in range(0, width, 256):
            cw = min(256, width - c0)
            y = jnp.dot(ub, w_ref[:, ws + c0: ws + c0 + cw], preferred_element_type=F32)
            for j in range(cw // LANE):
                blk = y[:, j * LANE:(j + 1) * LANE]
                if rope:
                    sw = jnp.where(first, pltpu.roll(blk, LANE - HEAD_DIM // 2, 1),
                                   pltpu.roll(blk, HEAD_DIM // 2, 1))
                    blk = blk * cos + sw * sin
                if scale != 1.0:
                    blk = blk * scale
                if act == "sigmoid":
                    blk = jax.nn.sigmoid(blk)
                o0 = os_ + c0 + j * LANE
                out_refs[oi][0, :, o0:o0 + LANE] = blk.astype(out_refs[oi].dtype)


def _norm_proj(h, gain, shift, scale, cos_t, sin_t, w, plan, outs, tm=512):
    b, s, d = h.shape
    n = w.shape[1]
    vec = pl.BlockSpec((1, 1, d), lambda bi, i: (bi, 0, 0))
    return pl.pallas_call(
        functools.partial(_proj_kernel, plan=tuple(plan)),
        grid=(b, s // tm),
        in_specs=[pl.BlockSpec((1, tm, d), lambda bi, i: (bi, i, 0)),
                  pl.BlockSpec((1, d), lambda bi, i: (0, 0)),
                  vec, vec,
                  pl.BlockSpec((1, tm, LANE), lambda bi, i: (bi, i, 0)),
                  pl.BlockSpec((1, tm, LANE), lambda bi, i: (bi, i, 0)),
                  pl.BlockSpec((d, n), lambda bi, i: (0, 0))],
        out_specs=[pl.BlockSpec((1, tm, wd), lambda bi, i: (bi, i, 0)) for wd, _ in outs],
        out_shape=[jax.ShapeDtypeStruct((b, s, wd), dt) for wd, dt in outs],
        compiler_params=_cparams("arbitrary", "arbitrary"),
        name="norm_proj",
    )(h, gain.reshape(1, d), shift, scale, cos_t, sin_t, w)


def _ordered_float(u):
    key = u ^ INT_MIN
    return lax.bitcast_convert_type(key ^ ((key >> 31) & np.int32(0x7FFFFFFF)), F32)


def _indexer_kernel(iq_ref, ik_ref, iw_ref, tri_ref, bias_ref, sc_scr, iqs_scr, iwb_scr, *, n_keep):
    i = pl.program_id(1)
    q0 = i * TQI
    s_len = bias_ref.shape[2]
    n_all = s_len // TK
    nch = (q0 + TQI + TK - 1) // TK
    t = q0 + lax.broadcasted_iota(jnp.int32, (TQI, 1), 0)
    iw = iw_ref[0]
    for hh in range(IDX_HEADS):
        iqs_scr[hh * TQI:(hh + 1) * TQI, :] = iq_ref[0, :, hh * LANE:(hh + 1) * LANE]
        iwb_scr[hh] = jnp.broadcast_to(iw[:, hh:hh + 1], (TQI, LANE))

    def lanes_of(c, j):
        return pl.ds(pl.multiple_of(c * TK + j * LANE, LANE), LANE)

    def score_chunk(c, carry):
        r = lax.dot_general(iqs_scr[...], ik_ref[0, pl.ds(pl.multiple_of(c * TK, TK), TK), :], _NT,
                            preferred_element_type=F32)
        pos = c * TK + lax.broadcasted_iota(jnp.int32, (1, LANE), 1)
        for j in range(TK // LANE):
            acc = jnp.zeros((TQI, LANE), F32)
            for hh in range(IDX_HEADS):
                acc = acc + iwb_scr[hh] * jnp.maximum(r[hh * TQI:(hh + 1) * TQI, j * LANE:(j + 1) * LANE], 0.0)
            sc_scr[:, lanes_of(c, j)] = jnp.where(pos + j * LANE <= t, acc, NEG_INF)
        return carry

    lax.fori_loop(0, nch, score_chunk, 0)

    def lane_sum(x):
        return jnp.sum(x, axis=1, keepdims=True)

    assert n_all * (TK // LANE) < 256
    b8, b16 = np.int32(1 << 8), np.int32(1 << 16)
    blocks = list(range(0, TQI, RB))
    thr_u = [jnp.zeros((RB, 1), jnp.int32) for _ in blocks]
    cnt_thr = [jnp.zeros((RB, 1), F32) + (nch * TK).astype(F32) for _ in blocks]
    for bit in range(30, -1, -2):
        cus = [[thr_u[bi] | np.array([m << bit], np.uint32).view(np.int32)[0] for m in (1, 2, 3)]
               for bi in range(len(blocks))]
        cbs = [[jnp.broadcast_to(_ordered_float(u), (RB, LANE)) for u in cu] for cu in cus]
        parts = []
        for bi, rb in enumerate(blocks):
            def body(c, part, cb=cbs[bi], rb=rb):
                for j in range(TK // LANE):
                    x = sc_scr[rb:rb + RB, lanes_of(c, j)]
                    part = part + jnp.where(x >= cb[2], 1 + b8 + b16,
                                            jnp.where(x >= cb[1], 1 + b8, jnp.where(x >= cb[0], 1, 0)))
                return part

            parts.append(lax.fori_loop(0, nch, body, jnp.zeros((RB, LANE), jnp.int32)))
        for bi in range(len(blocks)):
            for m, sh in enumerate((0, 8, 16)):
                cnt = lane_sum(((parts[bi] >> sh) & 0xFF).astype(F32))
                ok = cnt >= n_keep
                thr_u[bi] = jnp.where(ok, cus[bi][m], thr_u[bi])
                cnt_thr[bi] = jnp.where(ok, cnt, cnt_thr[bi])

    thr, has_thr, tied = [], [], None
    for bi in range(len(blocks)):
        f = _ordered_float(thr_u[bi])
        has = f > NEG_INF
        thr.append(jnp.where(has, f, NEG_INF))
        has_thr.append(has)
        any_tie = jnp.max(jnp.where(has & (cnt_thr[bi] > n_keep), 1, 0)) > 0
        tied = any_tie if tied is None else jnp.logical_or(tied, any_tie)

    @pl.when(jnp.logical_not(tied))
    def _():
        for bi, rb in enumerate(blocks):
            thr_b = jnp.broadcast_to(thr[bi], (RB, LANE))

            def body(c, carry, thr_b=thr_b, rb=rb):
                for j in range(TK // LANE):
                    x = sc_scr[rb:rb + RB, lanes_of(c, j)]
                    bias_ref[0, rb:rb + RB, lanes_of(c, j)] = (
                        jnp.where((x >= thr_b) & (x > NEG_INF), 0.0, NEG_INF).astype(BF16))
                return carry
            lax.fori_loop(0, nch, body, 0)

    @pl.when(tied)
    def _():
        for bi, rb in enumerate(blocks):
            th, has = thr[bi], has_thr[bi]

            def count_gt(c, part, th=th, rb=rb):
                x = sc_scr[rb:rb + RB, pl.ds(pl.multiple_of(c * TK, TK), TK)]
                return part + lane_sum(jnp.where(x > th, 1.0, 0.0))
            need = n_keep - lax.fori_loop(0, nch, count_gt, jnp.zeros((RB, 1), F32))

            def body(c, seen, th=th, has=has, need=need, rb=rb):
                x = sc_scr[rb:rb + RB, pl.ds(pl.multiple_of(c * TK, TK), TK)]
                tie = x == th
                pref = jnp.dot(jnp.where(tie, 1.0, 0.0).astype(BF16), tri_ref[...],
                               preferred_element_type=F32) + seen
                sel = ((x > th) | (tie & (pref <= need) & has)) & (x > NEG_INF)
                bias_ref[0, rb:rb + RB, pl.ds(pl.multiple_of(c * TK, TK), TK)] = (
                    jnp.where(sel, 0.0, NEG_INF).astype(BF16))
                return pref[:, TK - 1:TK]
            lax.fori_loop(0, nch, body, jnp.zeros((RB, 1), F32))

    def fill(c, carry):
        bias_ref[0, :, pl.ds(pl.multiple_of(c * TK, TK), TK)] = jnp.full((TQI, TK), NEG_INF, BF16)
        return carry
    lax.fori_loop(nch, n_all, fill, 0)


def _dsa_bias(proj, iw, s_len):
    b = proj.shape[0]
    n_keep = min(DSA_TOPK, s_len // 4)
    tri = jnp.asarray(np.triu(np.ones((TK, TK), np.float32)), BF16)
    return pl.pallas_call(
        functools.partial(_indexer_kernel, n_keep=n_keep),
        grid=(b, s_len // TQI),
        in_specs=[pl.BlockSpec((1, TQI, IDX_HEADS * LANE), lambda bi, i: (bi, i, 2)),
                  pl.BlockSpec((1, s_len, LANE), lambda bi, i: (bi, 0, 30)),
                  pl.BlockSpec((1, TQI, LANE), lambda bi, i: (bi, i, 0)),
                  pl.BlockSpec((TK, TK), lambda bi, i: (0, 0))],
        out_specs=pl.BlockSpec((1, TQI, s_len), lambda bi, i: (bi, i, 0)),
        out_shape=jax.ShapeDtypeStruct((b, s_len, s_len), BF16),
        scratch_shapes=[pltpu.VMEM((TQI, s_len), F32),
                        pltpu.VMEM((IDX_HEADS * TQI, LANE), BF16),
                        pltpu.VMEM((IDX_HEADS, TQI, LANE), F32)],
        compiler_params=_cparams("arbitrary", "arbitrary"),
        name="dsa_indexer",
    )(proj, proj, iw, tri)


def _stack_heads(q_ref, qs_scr):
    for g in range(N_KV_HEADS):
        for r in range(GROUP):
            hh = g * GROUP + r
            qs_scr[g, r * TQ:(r + 1) * TQ, :] = q_ref[0, :, hh * LANE:(hh + 1) * LANE]


def _online_update(s, v, m_scr, l_scr, acc_scr, g):
    m_old = m_scr[g]
    m_new = jnp.maximum(m_old, jnp.max(s, axis=1, keepdims=True))
    alpha = jnp.exp(m_old - m_new)
    p = jnp.exp(s - jnp.concatenate([m_new] * (s.shape[1] // LANE), axis=1))
    l_scr[g] = alpha * l_scr[g] + jnp.sum(p, axis=1, keepdims=True)
    acc_scr[g] = alpha * acc_scr[g] + jnp.dot(p.astype(BF16), v, preferred_element_type=F32)
    m_scr[g] = m_new


def _init_online(m_scr, l_scr, acc_scr):
    m_scr[...] = jnp.full(m_scr.shape, M_INIT, F32)
    l_scr[...] = jnp.zeros(l_scr.shape, F32)
    acc_scr[...] = jnp.zeros(acc_scr.shape, F32)


def _vhalf(g):
    return slice((g // 2) * LANE, (g // 2 + 1) * LANE)


def _dsa_attn_kernel(q_ref, k_ref, v_ref, bias_ref, o_ref, qs_scr, m_scr, l_scr, acc_scr):
    i = pl.program_id(1)
    nch = (i * TQ + TQ + TK - 1) // TK
    _stack_heads(q_ref, qs_scr)
    _init_online(m_scr, l_scr, acc_scr)

    def chunk(c, carry):
        k0 = pl.multiple_of(c * TK, TK)
        bias = bias_ref[0, :, pl.ds(k0, TK)].astype(F32)
        bias4 = jnp.concatenate([bias] * GROUP, axis=0)
        for g in range(N_KV_HEADS):
            kg = k_ref[0, pl.ds(k0, TK), g * LANE:(g + 1) * LANE]
            s = lax.dot_general(qs_scr[g], kg, _NT, preferred_element_type=F32) + bias4
            _online_update(s, v_ref[0, pl.ds(k0, TK), _vhalf(g)], m_scr, l_scr, acc_scr, g)
        return carry

    lax.fori_loop(0, nch, chunk, 0)
    for g in range(N_KV_HEADS):
        og = acc_scr[g] / l_scr[g]
        for r in range(GROUP):
            hh = g * GROUP + r
            o_ref[0, :, hh * LANE:(hh + 1) * LANE] = og[r * TQ:(r + 1) * TQ].astype(BF16)


def _dsa_attention(proj, bias):
    b, s_len, _ = proj.shape
    return pl.pallas_call(
        _dsa_attn_kernel,
        grid=(b, s_len // TQ),
        in_specs=[pl.BlockSpec((1, TQ, N_HEADS * LANE), lambda bi, i: (bi, i, 0)),
                  pl.BlockSpec((1, s_len, N_KV_HEADS * LANE), lambda bi, i: (bi, 0, 6)),
                  pl.BlockSpec((1, s_len, N_KV_HEADS * HEAD_DIM), lambda bi, i: (bi, 0, 14)),
                  pl.BlockSpec((1, TQ, s_len), lambda bi, i: (bi, i, 0))],
        out_specs=pl.BlockSpec((1, TQ, N_HEADS * LANE), lambda bi, i: (bi, i, 0)),
        out_shape=jax.ShapeDtypeStruct((b, s_len, N_HEADS * LANE), BF16),
        scratch_shapes=[pltpu.VMEM((N_KV_HEADS, GROUP * TQ, LANE), BF16),
                        pltpu.VMEM((N_KV_HEADS, GROUP * TQ, LANE), F32),
                        pltpu.VMEM((N_KV_HEADS, GROUP * TQ, LANE), F32),
                        pltpu.VMEM((N_KV_HEADS, GROUP * TQ, LANE), F32)],
        compiler_params=_cparams("arbitrary", "arbitrary"),
        name="dsa_attention",
    )(proj, proj, proj, bias)


def _outproj_kernel(o_ref, w_ref, gate_ref, h_ref, out_ref):
    y = jnp.dot(o_ref[0], w_ref[...], preferred_element_type=F32)
    out_ref[0] = h_ref[0] + gate_ref[0] * y


def _outproj_residual(o, w, gate, h, tm=512):
    b, s, d = h.shape
    k = o.shape[2]
    return pl.pallas_call(
        _outproj_kernel,
        grid=(b, s // tm),
        in_specs=[pl.BlockSpec((1, tm, k), lambda bi, i: (bi, i, 0)),
                  pl.BlockSpec((k, d), lambda bi, i: (0, 0)),
                  pl.BlockSpec((1, 1, d), lambda bi, i: (bi, 0, 0)),
                  pl.BlockSpec((1, tm, d), lambda bi, i: (bi, i, 0))],
        out_specs=pl.BlockSpec((1, tm, d), lambda bi, i: (bi, i, 0)),
        out_shape=jax.ShapeDtypeStruct((b, s, d), F32),
        compiler_params=_cparams("arbitrary", "arbitrary"),
        name="outproj_residual",
    )(o, w, gate, h)


def _ffn_kernel(h_ref, g_ref, sh_ref, sc_ref, gate_ref, wg_ref, wu_ref, wd_ref, out_ref, u_scr, acc_scr):
    f = pl.program_id(2)

    @pl.when(f == 0)
    def _():
        u_scr[...] = _norm_mod(h_ref[0], g_ref[...], sh_ref[0], sc_ref[0]).astype(BF16)
        acc_scr[...] = jnp.zeros(acc_scr.shape, F32)

    ub = u_scr[...]
    a = jnp.dot(ub, wg_ref[...], preferred_element_type=F32)
    up = jnp.dot(ub, wu_ref[...], preferred_element_type=F32)
    hid = (a * jax.nn.sigmoid(a)) * up
    acc_scr[...] += jnp.dot(hid.astype(BF16), wd_ref[...], preferred_element_type=F32)

    @pl.when(f == pl.num_programs(2) - 1)
    def _():
        out_ref[0] = h_ref[0] + gate_ref[0] * acc_scr[...]


def _ffn_residual(h, gain, shift, scale, gate, wg, wu, wd, tm=512, tf=1408):
    b, s, d = h.shape
    dff = wg.shape[1]
    vec = pl.BlockSpec((1, 1, d), lambda bi, i, f: (bi, 0, 0))
    return pl.pallas_call(
        _ffn_kernel,
        grid=(b, s // tm, dff // tf),
        in_specs=[pl.BlockSpec((1, tm, d), lambda bi, i, f: (bi, i, 0)),
                  pl.BlockSpec((1, d), lambda bi, i, f: (0, 0)),
                  vec, vec, vec,
                  pl.BlockSpec((d, tf), lambda bi, i, f: (0, f)),
                  pl.BlockSpec((d, tf), lambda bi, i, f: (0, f)),
                  pl.BlockSpec((tf, d), lambda bi, i, f: (f, 0))],
        out_specs=pl.BlockSpec((1, tm, d), lambda bi, i, f: (bi, i, 0)),
        out_shape=jax.ShapeDtypeStruct((b, s, d), F32),
        scratch_shapes=[pltpu.VMEM((tm, d), BF16), pltpu.VMEM((tm, d), F32)],
        compiler_params=_cparams("arbitrary", "arbitrary", "arbitrary"),
        name="ffn_residual",
    )(h, gain.reshape(1, d), shift, scale, gate, wg, wu, wd)


def _compress_kernel(x_ref, pe_ref, w1_ref, w2_ref, o_ref):
    x = x_ref[0, 0, 0]
    nr = x.shape[0]
    pa = jnp.dot((x + pe_ref[0, 0]).astype(BF16), w1_ref[0, 0], preferred_element_type=F32)
    pb = jnp.dot((x + pe_ref[0, 1]).astype(BF16), w1_ref[0, 1], preferred_element_type=F32)
    hid = pa + pltpu.roll(pb, nr - 1, 0)
    act = 0.5 * hid * (1.0 + jnp.tanh(np.float32(np.sqrt(2.0 / np.pi)) * (hid + 0.044715 * (hid * hid * hid))))
    o_ref[0, 0, 0] = jnp.dot(act.astype(BF16), w2_ref[0], preferred_element_type=F32).astype(BF16)


def _compress(x, pe, w1, w2):
    two, b, g, nr, kk = x.shape
    return pl.pallas_call(
        _compress_kernel,
        grid=(two, b, g),
        in_specs=[pl.BlockSpec((1, 1, 1, nr, kk), lambda a, bi, gi: (a, bi, gi, 0, 0)),
                  pl.BlockSpec((1, 2, 1, kk), lambda a, bi, gi: (a, 0, 0, 0)),
                  pl.BlockSpec((1, 2, kk, CMP_HIDDEN), lambda a, bi, gi: (a, 0, 0, 0)),
                  pl.BlockSpec((1, CMP_HIDDEN, HEAD_DIM), lambda a, bi, gi: (a, 0, 0))],
        out_specs=pl.BlockSpec((1, 1, 1, nr, HEAD_DIM), lambda a, bi, gi: (a, bi, gi, 0, 0)),
        out_shape=jax.ShapeDtypeStruct((two, b, g, nr, HEAD_DIM), BF16),
        compiler_params=_cparams("arbitrary", "arbitrary", "arbitrary"),
        name="nsa_compress",
    )(x, pe, w1, w2)


def _nsa_kernel(q_ref, gt_ref, kc_ref, vc_ref, ks_ref, kw_ref, vs_ref, vw_ref, aggt_ref, o_ref,
                kaug_scr, qs_scr, oc_scr, m_scr, l_scr, acc_scr, *, n_sel):
    i = pl.program_id(1)
    q0 = i * TQ
    nch = (q0 + TQ + TK - 1) // TK

    @pl.when(i == 0)
    def _():
        for c in range(kaug_scr.shape[0] // TK):
            row = c * TK + lax.broadcasted_iota(jnp.int32, (TK, N_KV_HEADS * LANE), 0)
            lane = lax.broadcasted_iota(jnp.int32, (TK, N_KV_HEADS * LANE), 1) & (LANE - 1)
            onehot = jnp.where(lane - HEAD_DIM == (row >> SLC_SHIFT), 1.0, 0.0).astype(BF16)
            kaug_scr[c * TK:(c + 1) * TK, :] = ks_ref[0, c * TK:(c + 1) * TK, :] + onehot

    _stack_heads(q_ref, qs_scr)
    _init_online(m_scr, l_scr, acc_scr)
    t = q0 + lax.broadcasted_iota(jnp.int32, (TQ, 1), 0)
    t4 = jnp.concatenate([t] * GROUP, axis=0)

    n_cmp = kc_ref.shape[1]
    cmp_end = lax.broadcasted_iota(jnp.int32, (1, n_cmp), 1) * CMP_STRIDE + (CMP_LEN - 1)
    cmask = cmp_end <= t4
    n_blk = aggt_ref.shape[0]
    jrow = lax.broadcasted_iota(jnp.int32, (n_blk, TQ), 0)
    tq = q0 + lax.broadcasted_iota(jnp.int32, (n_blk, TQ), 1)
    jt = tq >> SLC_SHIFT
    valid = jrow * SLC_LEN <= tq
    forced = (jrow == 0) | (jrow == jt) | (jrow == jt - 1)
    for g in range(N_KV_HEADS):
        sc = lax.dot_general(qs_scr[g], kc_ref[0, :, g * LANE:(g + 1) * LANE], _NT, preferred_element_type=F32)
        sm = jnp.where(cmask, sc, NEG_INF)
        e = jnp.where(cmask, jnp.exp(sm - jnp.max(sm, axis=1, keepdims=True)), 0.0)
        pc = e / jnp.maximum(jnp.sum(e, axis=1, keepdims=True), 1e-30)
        oc_scr[g] = jnp.dot(pc.astype(BF16), vc_ref[0, :, _vhalf(g)], preferred_element_type=F32)
        pcs = pc[0:TQ]
        for r in range(1, GROUP):
            pcs = pcs + pc[r * TQ:(r + 1) * TQ]
        imp = lax.dot_general(aggt_ref[...], pcs, _NT, preferred_element_type=F32,
                              precision=lax.Precision.HIGHEST)
        imp = jnp.where(valid, jnp.where(forced, POS_INF, imp), NEG_INF)
        rank = jnp.zeros((n_blk, TQ), jnp.int32)
        for j in range(n_blk):
            row = imp[j:j + 1, :]
            beats = (row > imp) | ((row == imp) & (jrow > j))
            rank = rank + beats.astype(jnp.int32)
        sel = (rank < n_sel) & valid
        selb = jnp.where(sel, 0.0, NEG_INF)
        pads = [jnp.zeros((HEAD_DIM, TQ), F32), selb]
        if n_blk < LANE - HEAD_DIM:
            pads.append(jnp.zeros((LANE - HEAD_DIM - n_blk, TQ), F32))
        selb = jnp.concatenate(pads, axis=0)
        selb = selb.T.astype(BF16)
        qs_scr[g] = qs_scr[g] + jnp.concatenate([selb] * GROUP, axis=0)

    def slc_chunk(c, diagonal):
        k0 = pl.multiple_of(c * TK, TK)
        for g in range(N_KV_HEADS):
            s = lax.dot_general(qs_scr[g], kaug_scr[pl.ds(k0, TK), g * LANE:(g + 1) * LANE], _NT,
                                preferred_element_type=F32)
            if diagonal:
                pos = k0 + lax.broadcasted_iota(jnp.int32, (1, TK), 1)
                s = jnp.where(pos <= t4, s, NEG_INF)
            _online_update(s, vs_ref[0, pl.ds(k0, TK), _vhalf(g)], m_scr, l_scr, acc_scr, g)

    def slc_body(c, carry):
        slc_chunk(c, False)
        return carry

    lax.fori_loop(0, nch - 1, slc_body, 0)
    slc_chunk(nch - 1, True)

    gt = gt_ref[0]
    w0 = pl.multiple_of(jnp.maximum(q0 - WINDOW, 0), TQ)
    wpos = w0 + lax.broadcasted_iota(jnp.int32, (1, WIN_SPAN), 1)
    wmask = (wpos <= t4) & (wpos > t4 - WINDOW)
    for g in range(N_KV_HEADS):
        s = lax.dot_general(qs_scr[g], kw_ref[0, pl.ds(w0, WIN_SPAN), g * LANE:(g + 1) * LANE], _NT,
                            preferred_element_type=F32)
        s = jnp.where(wmask, s, NEG_INF)
        p = jnp.exp(s - jnp.max(s, axis=1, keepdims=True))
        p = jnp.where(wmask, p, 0.0)
        ow = jnp.dot(p.astype(BF16), vw_ref[0, pl.ds(w0, WIN_SPAN), _vhalf(g)], preferred_element_type=F32)
        ow = ow / jnp.sum(p, axis=1, keepdims=True)
        osl = acc_scr[g] / l_scr[g]
        oc = oc_scr[g]
        for r in range(GROUP):
            hh = g * GROUP + r
            rs = slice(r * TQ, (r + 1) * TQ)
            o = (gt[:, hh:hh + 1] * oc[rs] + gt[:, N_HEADS + hh:N_HEADS + hh + 1] * osl[rs]
                 + gt[:, 2 * N_HEADS + hh:2 * N_HEADS + hh + 1] * ow[rs])
            o_ref[0, :, hh * LANE:(hh + 1) * LANE] = o.astype(BF16)


def _nsa_attention(q, gates, kc, vc, kv, agg_t):
    b, s_len, _ = q.shape
    n_cmp = kc.shape[1]
    n_blk = agg_t.shape[0]
    n_sel = min(SLC_TOPN, s_len // SLC_LEN)
    rows = GROUP * TQ
    return pl.pallas_call(
        functools.partial(_nsa_kernel, n_sel=n_sel),
        grid=(b, s_len // TQ),
        in_specs=[pl.BlockSpec((1, TQ, N_HEADS * LANE), lambda bi, i: (bi, i, 0)),
                  pl.BlockSpec((1, TQ, LANE), lambda bi, i: (bi, i, 0)),
                  pl.BlockSpec((1, n_cmp, N_KV_HEADS * LANE), lambda bi, i: (bi, 0, 0)),
                  pl.BlockSpec((1, n_cmp, N_KV_HEADS * HEAD_DIM), lambda bi, i: (bi, 0, 0)),
                  pl.BlockSpec((1, s_len, N_KV_HEADS * LANE), lambda bi, i: (bi, 0, 0)),
                  pl.BlockSpec((1, s_len, N_KV_HEADS * LANE), lambda bi, i: (bi, 0, 1)),
                  pl.BlockSpec((1, s_len, N_KV_HEADS * HEAD_DIM), lambda bi, i: (bi, 0, 4)),
                  pl.BlockSpec((1, s_len, N_KV_HEADS * HEAD_DIM), lambda bi, i: (bi, 0, 5)),
                  pl.BlockSpec((n_blk, n_cmp), lambda bi, i: (0, 0))],
        out_specs=pl.BlockSpec((1, TQ, N_HEADS * LANE), lambda bi, i: (bi, i, 0)),
        out_shape=jax.ShapeDtypeStruct((b, s_len, N_HEADS * LANE), BF16),
        scratch_shapes=[pltpu.VMEM((s_len, N_KV_HEADS * LANE), BF16),
                        pltpu.VMEM((N_KV_HEADS, rows, LANE), BF16),
                        pltpu.VMEM((N_KV_HEADS, rows, LANE), F32),
                        pltpu.VMEM((N_KV_HEADS, rows, LANE), F32),
                        pltpu.VMEM((N_KV_HEADS, rows, LANE), F32),
                        pltpu.VMEM((N_KV_HEADS, rows, LANE), F32)],
        compiler_params=_cparams("arbitrary", "arbitrary"),
        name="nsa_attention",
    )(q, gates, kc, vc, kv, kv, kv, kv, agg_t)


def _moe_kernel(h_ref, g_ref, sh_ref, sc_ref, gate_ref, wr_ref, fg_ref, wg_ref, wu_ref, wd_ref, out_ref,
                u_scr, rw_scr, acc_scr):
    e = pl.program_id(2)
    f = pl.program_id(3)

    @pl.when((e == 0) & (f == 0))
    def _():
        u = _norm_mod(h_ref[0], g_ref[...], sh_ref[0], sc_ref[0])
        u_scr[...] = u.astype(BF16)
        acc_scr[...] = jnp.zeros(acc_scr.shape, F32)
        logits = jnp.dot(u, wr_ref[...], preferred_element_type=F32, precision=lax.Precision.HIGHEST)
        lane = lax.broadcasted_iota(jnp.int32, logits.shape, 1)
        logits = jnp.where(lane < N_EXPERTS, logits, -jnp.inf)
        m1 = jnp.max(logits, axis=1, keepdims=True)
        i1 = jnp.min(jnp.where(logits == m1, lane, LANE), axis=1, keepdims=True)
        rest = jnp.where(lane == i1, -jnp.inf, logits)
        m2 = jnp.max(rest, axis=1, keepdims=True)
        i2 = jnp.min(jnp.where(rest == m2, lane, LANE), axis=1, keepdims=True)
        e2 = jnp.exp(m2 - m1)
        w1 = 1.0 / (1.0 + e2)
        w2 = e2 / (1.0 + e2)
        rw_scr[...] = jnp.where(lane == i1, w1, 0.0) + jnp.where(lane == i2, w2, 0.0)

    ub = u_scr[...]
    a = jnp.dot(ub, wg_ref[0], preferred_element_type=F32)
    up = jnp.dot(ub, wu_ref[0], preferred_element_type=F32)
    hid = (a * jax.nn.sigmoid(a)) * up
    y = jnp.dot(hid.astype(BF16), wd_ref[0], preferred_element_type=F32)
    rw = rw_scr[...]
    lane = lax.broadcasted_iota(jnp.int32, rw.shape, 1)
    we = jnp.sum(jnp.where(lane == e, rw, 0.0), axis=1, keepdims=True)
    acc_scr[...] += we * y

    @pl.when((e == pl.num_programs(2) - 1) & (f == pl.num_programs(3) - 1))
    def _():
        hn = h_ref[0] + gate_ref[0] * acc_scr[...]
        ms = jnp.mean(hn * hn, axis=-1, keepdims=True)
        out_ref[0] = (hn * lax.rsqrt(ms + RMS_EPS)) * fg_ref[...]


def _moe_residual_norm(h, gain, shift, scale, gate, w_router, final_gain, wg, wu, wd, tm=512, tf=512):
    b, s, d = h.shape
    ne, _, dff = wg.shape
    vec = pl.BlockSpec((1, 1, d), lambda bi, i, e, f: (bi, 0, 0))
    row = pl.BlockSpec((1, d), lambda bi, i, e, f: (0, 0))
    return pl.pallas_call(
        _moe_kernel,
        grid=(b, s // tm, ne, dff // tf),
        in_specs=[pl.BlockSpec((1, tm, d), lambda bi, i, e, f: (bi, i, 0)),
                  row, vec, vec, vec,
                  pl.BlockSpec((d, LANE), lambda bi, i, e, f: (0, 0)),
                  row,
                  pl.BlockSpec((1, d, tf), lambda bi, i, e, f: (e, 0, f)),
                  pl.BlockSpec((1, d, tf), lambda bi, i, e, f: (e, 0, f)),
                  pl.BlockSpec((1, tf, d), lambda bi, i, e, f: (e, f, 0))],
        out_specs=pl.BlockSpec((1, tm, d), lambda bi, i, e, f: (bi, i, 0)),
        out_shape=jax.ShapeDtypeStruct((b, s, d), F32),
        scratch_shapes=[pltpu.VMEM((tm, d), BF16), pltpu.VMEM((tm, LANE), F32), pltpu.VMEM((tm, d), F32)],
        compiler_params=_cparams("arbitrary", "arbitrary", "arbitrary", "arbitrary"),
        name="moe_residual_norm",
    )(h, gain.reshape(1, d), shift, scale, gate, w_router, final_gain.reshape(1, d), wg, wu, wd)


def _pad_heads(w, n_heads):
    d = w.shape[0]
    w = w.reshape(d, n_heads, HEAD_DIM)
    return jnp.pad(w, ((0, 0), (0, 0), (0, LANE - HEAD_DIM))).reshape(d, n_heads * LANE)


def _pad_cols(w, n):
    return jnp.pad(w, ((0, 0), (0, n - w.shape[1])))


def _out_rows(w_out):
    d = w_out.shape[1]
    w = w_out.reshape(N_KV_HEADS, GROUP, HEAD_DIM, d)
    z = jnp.zeros_like(w)
    even = jnp.concatenate([w, z], axis=2)
    odd = jnp.concatenate([z, w], axis=2)
    sel = (np.arange(N_KV_HEADS) % 2 == 0)[:, None, None, None]
    return jnp.where(sel, even, odd).reshape(N_HEADS * LANE, d)


def _dsa_weights(w_in):
    q, k, v, iq, ik, iw = jnp.split(w_in, [1024, 1280, 1536, 2048, 2112], axis=1)
    w = jnp.concatenate([_pad_heads(q, N_HEADS), _pad_heads(iq, IDX_HEADS), _pad_heads(k, N_KV_HEADS), v,
                         _pad_cols(ik, LANE), _pad_cols(iw, LANE)], axis=1).astype(BF16)
    plan = [(0, 2048, True, HEAD_DIM ** -0.5, 0, 0, None),
            (2048, 1024, True, 1.0, 0, 2048, None),
            (3072, 512, True, 1.0, 0, 3072, None),
            (3584, 256, False, 1.0, 0, 3584, None),
            (3840, 128, True, 1.0, 0, 3840, None),
            (3968, 128, False, IDX_HEADS ** -0.5 * IDX_DIM ** -0.5, 1, 0, None)]
    return w, plan, [(3968, BF16), (LANE, F32)]


def _nsa_q_weights(w_q):
    q = w_q[:, :N_HEADS * HEAD_DIM]
    gates = w_q[:, N_HEADS * HEAD_DIM:].reshape(-1, N_HEADS, N_NSA_BRANCH)
    gates = jnp.transpose(gates, (0, 2, 1)).reshape(-1, N_NSA_BRANCH * N_HEADS)
    w = jnp.concatenate([_pad_heads(q, N_HEADS), _pad_cols(gates, LANE)], axis=1).astype(BF16)
    plan = [(0, 2048, True, HEAD_DIM ** -0.5, 0, 0, None),
            (2048, 128, False, 1.0, 1, 0, "sigmoid")]
    return w, plan, [(2048, BF16), (LANE, F32)]


def _kv_weights(w_kv):
    kvw = w_kv.reshape(-1, 2 * N_NSA_BRANCH, N_KV_HEADS * HEAD_DIM)
    k_cmp, v_cmp, k_slc, v_slc, k_win, v_win = [kvw[:, j] for j in range(2 * N_NSA_BRANCH)]
    w = jnp.concatenate([_pad_heads(k_slc, N_KV_HEADS), _pad_heads(k_win, N_KV_HEADS), v_slc, v_win,
                         k_cmp, v_cmp], axis=1).astype(BF16)
    plan = [(0, 512, True, 1.0, 0, 0, None),
            (512, 512, True, 1.0, 0, 512, None),
            (1024, 256, False, 1.0, 0, 1024, None),
            (1280, 256, False, 1.0, 0, 1280, None),
            (1536, 256, True, 1.0, 1, 0, None),
            (1792, 256, False, 1.0, 1, 256, None)]
    return w, plan, [(1536, BF16), (512, F32)]


def _nsa_constants(s_len):
    n_cmp = (s_len - CMP_LEN) // CMP_STRIDE + 1
    n_slc = s_len // SLC_LEN
    cmp_start = np.arange(n_cmp) * CMP_STRIDE
    slc_start = np.arange(n_slc) * SLC_LEN
    ov = (np.minimum(cmp_start[:, None] + CMP_LEN, slc_start[None, :] + SLC_LEN)
          - np.maximum(cmp_start[:, None], slc_start[None, :]))
    agg = (np.clip(ov, 0, None) / CMP_LEN).astype(np.float32)
    agg_t = np.zeros((n_slc, s_len // CMP_STRIDE), np.float32)
    agg_t[:, :n_cmp] = agg.T
    return jnp.asarray(agg_t)


def kernel(x, c, positions, attn_gain, ffn_gain, w_ada, b_ada, a_w_in, a_w_out, b_w_q, b_w_out, kv_gain, w_kv_ada, b_kv_ada, w_kv, cmp_pe_k, cmp_w1_k, cmp_w2_k, cmp_pe_v, cmp_w1_v, cmp_w2_v, ffn_w_gate, ffn_w_up, ffn_w_down, moe_w_router, moe_w_gate, moe_w_up, moe_w_down, final_gain):
    b, s_len, d = x.shape
    assert s_len % TK == 0 and s_len // SLC_LEN <= HEAD_DIM and s_len >= WIN_SPAN and b <= 8

    cos_t, sin_t = _rope_tables(positions)
    c_pad = jnp.pad(c, ((0, 8 - b), (0, 0)))
    mod = _ada(c_pad, w_ada, b_ada)[:, :b]
    kv_mod = _ada(c_pad, w_kv_ada[None], b_kv_ada[None])[0, :b]

    def parts(m, n):
        return [p[:, None, :] for p in jnp.split(m, n, axis=-1)]

    a_shift, a_scale, a_gate, f_shift, f_scale, f_gate = parts(mod[0], 6)
    w, plan, outs = _dsa_weights(a_w_in[0])
    proj, iw = _norm_proj(x, attn_gain[0], a_shift, a_scale, cos_t, sin_t, w, plan, outs)
    bias = _dsa_bias(proj, iw, s_len)
    o = _dsa_attention(proj, bias)
    h = _outproj_residual(o, _out_rows(a_w_out[0]).astype(BF16), a_gate, x)
    h = _ffn_residual(h, ffn_gain[0], f_shift, f_scale, f_gate,
                      ffn_w_gate[0].astype(BF16), ffn_w_up[0].astype(BF16), ffn_w_down[0].astype(BF16))

    kv_shift, kv_scale = parts(kv_mod, 2)
    w, plan, outs = _kv_weights(w_kv)
    kv, kv_cmp = _norm_proj(h, kv_gain, kv_shift, kv_scale, cos_t, sin_t, w, plan, outs)
    nr = s_len // CMP_STRIDE
    xc = kv_cmp.reshape(b, s_len, 2, N_KV_HEADS, HEAD_DIM)
    xc = jnp.transpose(xc, (2, 0, 3, 1, 4)).reshape(2, b, N_KV_HEADS, nr, CMP_STRIDE * HEAD_DIM)
    pe = jnp.stack([cmp_pe_k, cmp_pe_v]).reshape(2, 2, 1, CMP_STRIDE * HEAD_DIM)
    w1 = jnp.stack([cmp_w1_k, cmp_w1_v]).reshape(2, 2, CMP_STRIDE * HEAD_DIM, CMP_HIDDEN).astype(BF16)
    w2 = jnp.stack([cmp_w2_k, cmp_w2_v]).astype(BF16)
    cmp = _compress(xc, pe, w1, w2)
    cmp = jnp.transpose(cmp, (0, 1, 3, 2, 4))
    kc = jnp.pad(cmp[0], ((0, 0), (0, 0), (0, 0), (0, LANE - HEAD_DIM))).reshape(b, nr, N_KV_HEADS * LANE)
    vc = cmp[1].reshape(b, nr, N_KV_HEADS * HEAD_DIM)

    a_shift, a_scale, a_gate, f_shift, f_scale, f_gate = parts(mod[1], 6)
    w, plan, outs = _nsa_q_weights(b_w_q[0])
    q, gates = _norm_proj(h, attn_gain[1], a_shift, a_scale, cos_t, sin_t, w, plan, outs)
    o = _nsa_attention(q, gates, kc, vc, kv, _nsa_constants(s_len))
    h = _outproj_residual(o, _out_rows(b_w_out[0]).astype(BF16), a_gate, h)
    return _moe_residual_norm(h, ffn_gain[1], f_shift, f_scale, f_gate, _pad_cols(moe_w_router[0], LANE),
                              final_gain, moe_w_gate[0].astype(BF16), moe_w_up[0].astype(BF16),
                              moe_w_down[0].astype(BF16))
```

```python
import functools

import numpy as np
import jax
import jax.numpy as jnp
from jax import lax
from jax.experimental import pallas as pl
from jax.experimental.pallas import tpu as pltpu

F32 = jnp.float32
BF16 = jnp.bfloat16

D_MODEL = 1024
HEAD_DIM = 64
N_HEADS = 16
N_KV_HEADS = 4
GROUP = N_HEADS // N_KV_HEADS
ROPE_THETA = 10000.0
RMS_EPS = 1e-6
NEG_INF = -1e30
MASKED = float("-inf")
POS_INF = 1e30
IDX_HEADS = 8
IDX_DIM = HEAD_DIM
DSA_TOPK = 256
CMP_LEN = 32
CMP_STRIDE = 16
CMP_HIDDEN = 256
SLC_LEN = 64
SLC_SHIFT = 6
SLC_TOPN = 16
WINDOW = 512
N_NSA_BRANCH = 3
N_EXPERTS = 8
TOP_K_EXPERTS = 2

LANE = 128
TQ = 256
TQI = 512
RB = 64
TK = 512
WIN_SPAN = WINDOW + TQ
TMX = 512
RBLK = 256
GRP = 512
INT_MIN = np.int32(-2 ** 31)
M_INIT = -3.0e38
Q_SCALE = HEAD_DIM ** -0.5 * float(np.log2(np.e))
VMEM_LIMIT = 56 * 1024 * 1024

_NT = (((1,), (1,)), ((), ()))


def _cparams(*sem, flags=None):
    return pltpu.CompilerParams(dimension_semantics=sem, vmem_limit_bytes=VMEM_LIMIT, flags=flags)


def _rope_kernel(pos_ref, inv_ref, sgn_ref, cos_ref, sin_ref):
    ang = pos_ref[0].astype(F32) * inv_ref[...]
    cos_ref[0] = jnp.cos(ang)
    sin_ref[0] = jnp.sin(ang) * sgn_ref[...]


def _rope_tables(positions):
    b, s = positions.shape
    inv = 1.0 / (ROPE_THETA ** (jnp.arange(0, HEAD_DIM, 2, dtype=F32) / HEAD_DIM))
    inv = jnp.tile(inv, LANE // (HEAD_DIM // 2))[None]
    sgn = np.where((np.arange(LANE) % HEAD_DIM) < HEAD_DIM // 2, -1.0, 1.0).astype(np.float32)[None]
    return pl.pallas_call(
        _rope_kernel,
        grid=(b,),
        in_specs=[pl.BlockSpec((1, s, 1), lambda i: (i, 0, 0)),
                  pl.BlockSpec((1, LANE), lambda i: (0, 0)),
                  pl.BlockSpec((1, LANE), lambda i: (0, 0))],
        out_specs=[pl.BlockSpec((1, s, LANE), lambda i: (i, 0, 0))] * 2,
        out_shape=[jax.ShapeDtypeStruct((b, s, LANE), F32)] * 2,
        compiler_params=_cparams("arbitrary"),
        name="rope_tables",
    )(positions[..., None], inv, jnp.asarray(sgn))


def _ada_kernel(c_ref, w_ref, b_ref, o_ref):
    c = c_ref[...]
    ca = c * jax.nn.sigmoid(c)
    o_ref[0] = jnp.dot(ca, w_ref[0], preferred_element_type=F32,
                       precision=lax.Precision.HIGHEST) + b_ref[0]


def _ada(c_pad, w, bias):
    nl, d, n = w.shape
    tn = 1024
    return pl.pallas_call(
        _ada_kernel,
        grid=(nl, n // tn),
        in_specs=[pl.BlockSpec((8, d), lambda l, j: (0, 0)),
                  pl.BlockSpec((1, d, tn), lambda l, j: (l, 0, j)),
                  pl.BlockSpec((1, 1, tn), lambda l, j: (l, 0, j))],
        out_specs=pl.BlockSpec((1, 8, tn), lambda l, j: (l, 0, j)),
        out_shape=jax.ShapeDtypeStruct((nl, 8, n), F32),
        compiler_params=_cparams("arbitrary", "arbitrary"),
        name="ada_mod",
    )(c_pad, w, bias.reshape(nl, 1, n))


def _norm_mod(x, g, shift, scale):
    ms = jnp.mean(x * x, axis=-1, keepdims=True)
    u = (x * lax.rsqrt(ms + RMS_EPS)) * g
    return u * (1.0 + scale) + shift


def _proj_kernel(h_ref, g_ref, sh_ref, sc_ref, cos_ref, sin_ref, w_ref, *out_refs, plan):
    ub = _norm_mod(h_ref[0], g_ref[...], sh_ref[0], sc_ref[0]).astype(BF16)
    cos = cos_ref[0]
    sin = sin_ref[0]
    lane = lax.broadcasted_iota(jnp.int32, cos.shape, 1)
    first = (lane & (HEAD_DIM - 1)) < HEAD_DIM // 2
    for (ws, width, rope, scale, oi, os_, act) in plan:
        for c0 in range(0, width, 256):
            cw = min(256, width - c0)
            y = jnp.dot(ub, w_ref[:, ws + c0: ws + c0 + cw], preferred_element_type=F32)
            for j in range(cw // LANE):
                blk = y[:, j * LANE:(j + 1) * LANE]
                if rope:
                    sw = jnp.where(first, pltpu.roll(blk, LANE - HEAD_DIM // 2, 1),
                                   pltpu.roll(blk, HEAD_DIM // 2, 1))
                    blk = blk * cos + sw * sin
                if scale != 1.0:
                    blk = blk * scale
                if act == "sigmoid":
                    blk = jax.nn.sigmoid(blk)
                elif act == "ones":
                    blk = jnp.where(lane == HEAD_DIM, 1.0, blk)
                elif act == "blockhot":
                    tok = pl.program_id(1) * cos.shape[0] + lax.broadcasted_iota(jnp.int32, cos.shape, 0)
                    blk = jnp.where(lane - HEAD_DIM == (tok >> SLC_SHIFT), 1.0, blk)
                o0 = os_ + c0 + j * LANE
                out_refs[oi][0, :, o0:o0 + LANE] = blk.astype(out_refs[oi].dtype)


def _norm_proj(h, gain, shift, scale, cos_t, sin_t, w, plan, outs, tm=512):
    b, s, d = h.shape
    n = w.shape[1]
    vec = pl.BlockSpec((1, 1, d), lambda bi, i: (bi, 0, 0))
    return pl.pallas_call(
        functools.partial(_proj_kernel, plan=tuple(plan)),
        grid=(b, s // tm),
        in_specs=[pl.BlockSpec((1, tm, d), lambda bi, i: (bi, i, 0)),
                  pl.BlockSpec((1, d), lambda bi, i: (0, 0)),
                  vec, vec,
                  pl.BlockSpec((1, tm, LANE), lambda bi, i: (bi, i, 0)),
                  pl.BlockSpec((1, tm, LANE), lambda bi, i: (bi, i, 0)),
                  pl.BlockSpec((d, n), lambda bi, i: (0, 0))],
        out_specs=[pl.BlockSpec((1, tm, wd), lambda bi, i: (bi, i, 0)) for wd, _ in outs],
        out_shape=[jax.ShapeDtypeStruct((b, s, wd), dt) for wd, dt in outs],
        compiler_params=_cparams("arbitrary", "arbitrary"),
        name="norm_proj",
    )(h, gain.reshape(1, d), shift, scale, cos_t, sin_t, w)


def _ordered_float(u):
    key = u ^ INT_MIN
    return lax.bitcast_convert_type(key ^ ((key >> 31) & np.int32(0x7FFFFFFF)), F32)


def _indexer_kernel(iq_ref, ik_ref, iw_ref, tri_ref, bias_ref, sc_scr, iqs_scr, iwb_scr, *, n_keep):
    i = pl.program_id(1)
    q0 = i * TQI
    s_len = bias_ref.shape[2]
    n_all = s_len // TK
    nch = (q0 + TQI + TK - 1) // TK
    t = q0 + lax.broadcasted_iota(jnp.int32, (TQI, 1), 0)
    iw = iw_ref[0]
    for hh in range(IDX_HEADS):
        iqs_scr[hh * TQI:(hh + 1) * TQI, :] = iq_ref[0, :, hh * LANE:(hh + 1) * LANE]
        iwb_scr[hh] = jnp.broadcast_to(iw[:, hh:hh + 1], (TQI, LANE))

    def lanes_of(c, j):
        return pl.ds(pl.multiple_of(c * TK + j * LANE, LANE), LANE)

    def score_chunk(c, carry):
        r = lax.dot_general(iqs_scr[...], ik_ref[0, pl.ds(pl.multiple_of(c * TK, TK), TK), :], _NT,
                            preferred_element_type=F32)
        pos = c * TK + lax.broadcasted_iota(jnp.int32, (1, LANE), 1)
        for j in range(TK // LANE):
            acc = jnp.zeros((TQI, LANE), F32)
            for hh in range(IDX_HEADS):
                acc = acc + iwb_scr[hh] * jnp.maximum(r[hh * TQI:(hh + 1) * TQI, j * LANE:(j + 1) * LANE], 0.0)
            sc_scr[:, lanes_of(c, j)] = jnp.where(pos + j * LANE <= t, acc, MASKED)
        return carry

    lax.fori_loop(0, nch, score_chunk, 0)

    def lane_sum(x):
        return jnp.sum(x, axis=1, keepdims=True)

    assert n_all * (TK // LANE) < 256
    b8, b16 = np.int32(1 << 8), np.int32(1 << 16)
    blocks = list(range(0, TQI, RB))
    thr_u = [jnp.zeros((RB, 1), jnp.int32) for _ in blocks]
    cnt_thr = [jnp.zeros((RB, 1), F32) + (nch * TK).astype(F32) for _ in blocks]
    for bit in range(30, -1, -2):
        cus = [[thr_u[bi] | np.array([m << bit], np.uint32).view(np.int32)[0] for m in (1, 2, 3)]
               for bi in range(len(blocks))]
        cbs = [[jnp.broadcast_to(_ordered_float(u), (RB, LANE)) for u in cu] for cu in cus]
        parts = []
        for bi, rb in enumerate(blocks):
            def body(c, part, cb=cbs[bi], rb=rb):
                for j in range(TK // LANE):
                    x = sc_scr[rb:rb + RB, lanes_of(c, j)]
                    part = part + jnp.where(x >= cb[2], 1 + b8 + b16,
                                            jnp.where(x >= cb[1], 1 + b8, jnp.where(x >= cb[0], 1, 0)))
                return part

            parts.append(lax.fori_loop(0, nch, body, jnp.zeros((RB, LANE), jnp.int32)))
        for bi in range(len(blocks)):
            for m, sh in enumerate((0, 8, 16)):
                cnt = lane_sum(((parts[bi] >> sh) & 0xFF).astype(F32))
                ok = cnt >= n_keep
                thr_u[bi] = jnp.where(ok, cus[bi][m], thr_u[bi])
                cnt_thr[bi] = jnp.where(ok, cnt, cnt_thr[bi])

    thr, has_thr, tied = [], [], None
    for bi in range(len(blocks)):
        f = _ordered_float(thr_u[bi])
        has = f > MASKED
        thr.append(jnp.where(has, f, MASKED))
        has_thr.append(has)
        any_tie = jnp.max(jnp.where(has & (cnt_thr[bi] > n_keep), 1, 0)) > 0
        tied = any_tie if tied is None else jnp.logical_or(tied, any_tie)

    @pl.when(jnp.logical_not(tied))
    def _():
        for bi, rb in enumerate(blocks):
            thr_b = jnp.broadcast_to(thr[bi], (RB, LANE))

            def body(c, carry, thr_b=thr_b, rb=rb):
                for j in range(TK // LANE):
                    x = sc_scr[rb:rb + RB, lanes_of(c, j)]
                    bias_ref[0, rb:rb + RB, lanes_of(c, j)] = (
                        jnp.where((x >= thr_b) & (x > MASKED), 0.0, NEG_INF).astype(BF16))
                return carry
            lax.fori_loop(0, nch, body, 0)

    @pl.when(tied)
    def _():
        for bi, rb in enumerate(blocks):
            th, has = thr[bi], has_thr[bi]

            def count_gt(c, part, th=th, rb=rb):
                x = sc_scr[rb:rb + RB, pl.ds(pl.multiple_of(c * TK, TK), TK)]
                return part + lane_sum(jnp.where(x > th, 1.0, 0.0))
            need = n_keep - lax.fori_loop(0, nch, count_gt, jnp.zeros((RB, 1), F32))

            def body(c, seen, th=th, has=has, need=need, rb=rb):
                x = sc_scr[rb:rb + RB, pl.ds(pl.multiple_of(c * TK, TK), TK)]
                tie = x == th
                pref = jnp.dot(jnp.where(tie, 1.0, 0.0).astype(BF16), tri_ref[...],
                               preferred_element_type=F32) + seen
                sel = ((x > th) | (tie & (pref <= need) & has)) & (x > MASKED)
                bias_ref[0, rb:rb + RB, pl.ds(pl.multiple_of(c * TK, TK), TK)] = (
                    jnp.where(sel, 0.0, NEG_INF).astype(BF16))
                return pref[:, TK - 1:TK]
            lax.fori_loop(0, nch, body, jnp.zeros((RB, 1), F32))

    def fill(c, carry):
        bias_ref[0, :, pl.ds(pl.multiple_of(c * TK, TK), TK)] = jnp.full((TQI, TK), NEG_INF, BF16)
        return carry
    lax.fori_loop(nch, n_all, fill, 0)


def _dsa_bias(proj, iw, s_len):
    b = proj.shape[0]
    n_keep = min(DSA_TOPK, s_len // 4)
    tri = jnp.asarray(np.triu(np.ones((TK, TK), np.float32)), BF16)
    return pl.pallas_call(
        functools.partial(_indexer_kernel, n_keep=n_keep),
        grid=(b, s_len // TQI),
        in_specs=[pl.BlockSpec((1, TQI, IDX_HEADS * LANE), lambda bi, i: (bi, i, 2)),
                  pl.BlockSpec((1, s_len, LANE), lambda bi, i: (bi, 0, 32)),
                  pl.BlockSpec((1, TQI, LANE), lambda bi, i: (bi, i, 0)),
                  pl.BlockSpec((TK, TK), lambda bi, i: (0, 0))],
        out_specs=pl.BlockSpec((1, TQI, s_len), lambda bi, i: (bi, i, 0)),
        out_shape=jax.ShapeDtypeStruct((b, s_len, s_len), BF16),
        scratch_shapes=[pltpu.VMEM((TQI, s_len), F32),
                        pltpu.VMEM((IDX_HEADS * TQI, LANE), BF16),
                        pltpu.VMEM((IDX_HEADS, TQI, LANE), F32)],
        compiler_params=_cparams("arbitrary", "arbitrary"),
        name="dsa_indexer",
    )(proj, proj, iw, tri)


def _stack_heads(q_ref, qs_scr):
    for g in range(N_KV_HEADS):
        for r in range(GROUP):
            hh = g * GROUP + r
            qs_scr[g, r * TQ:(r + 1) * TQ, :] = q_ref[0, :, hh * LANE:(hh + 1) * LANE]


def _online_update(s, v, m_scr, acc_scr, g):
    m_old = m_scr[g]
    m_new = jnp.maximum(m_old, jnp.max(s, axis=1, keepdims=True))
    alpha = jnp.exp2(m_old - m_new)
    p = jnp.exp2(s - jnp.concatenate([m_new] * (s.shape[1] // LANE), axis=1))
    acc_scr[g] = alpha * acc_scr[g] + jnp.dot(p.astype(BF16), v, preferred_element_type=F32)
    m_scr[g] = m_new


def _init_online(m_scr, acc_scr):
    m_scr[...] = jnp.full(m_scr.shape, M_INIT, F32)
    acc_scr[...] = jnp.zeros(acc_scr.shape, F32)


def _normalised(acc):
    return acc / acc[:, HEAD_DIM:HEAD_DIM + 1]


def _head(g):
    return slice(g * LANE, (g + 1) * LANE)


def _dsa_attn_kernel(q_ref, k_ref, v_ref, bias_ref, o_ref, qs_scr, m_scr, acc_scr):
    i = pl.program_id(1)
    nch = (i * TQ + TQ + TK - 1) // TK
    _stack_heads(q_ref, qs_scr)
    _init_online(m_scr, acc_scr)

    def chunk(c, carry):
        k0 = pl.multiple_of(c * TK, TK)
        bias = bias_ref[0, :, pl.ds(k0, TK)].astype(F32)
        bias4 = jnp.concatenate([bias] * GROUP, axis=0)
        for g in range(N_KV_HEADS):
            s = lax.dot_general(qs_scr[g], k_ref[0, pl.ds(k0, TK), _head(g)], _NT,
                                preferred_element_type=F32) + bias4
            _online_update(s, v_ref[0, pl.ds(k0, TK), _head(g)], m_scr, acc_scr, g)
        return carry

    lax.fori_loop(0, nch, chunk, 0)
    for g in range(N_KV_HEADS):
        og = _normalised(acc_scr[g])
        for r in range(GROUP):
            hh = g * GROUP + r
            o_ref[0, :, hh * LANE:(hh + 1) * LANE] = og[r * TQ:(r + 1) * TQ].astype(BF16)


def _dsa_attention(proj, bias):
    b, s_len, _ = proj.shape
    return pl.pallas_call(
        _dsa_attn_kernel,
        grid=(b, s_len // TQ),
        in_specs=[pl.BlockSpec((1, TQ, N_HEADS * LANE), lambda bi, i: (bi, i, 0)),
                  pl.BlockSpec((1, s_len, N_KV_HEADS * LANE), lambda bi, i: (bi, 0, 6)),
                  pl.BlockSpec((1, s_len, N_KV_HEADS * LANE), lambda bi, i: (bi, 0, 7)),
                  pl.BlockSpec((1, TQ, s_len), lambda bi, i: (bi, i, 0))],
        out_specs=pl.BlockSpec((1, TQ, N_HEADS * LANE), lambda bi, i: (bi, i, 0)),
        out_shape=jax.ShapeDtypeStruct((b, s_len, N_HEADS * LANE), BF16),
        scratch_shapes=[pltpu.VMEM((N_KV_HEADS, GROUP * TQ, LANE), BF16),
                        pltpu.VMEM((N_KV_HEADS, GROUP * TQ, LANE), F32),
                        pltpu.VMEM((N_KV_HEADS, GROUP * TQ, LANE), F32)],
        compiler_params=_cparams("arbitrary", "arbitrary"),
        name="dsa_attention",
    )(proj, proj, proj, bias)


def _outproj_kernel(o_ref, w_ref, gate_ref, h_ref, out_ref):
    y = jnp.dot(o_ref[0], w_ref[...], preferred_element_type=F32)
    out_ref[0] = h_ref[0] + gate_ref[0] * y


def _outproj_residual(o, w, gate, h, tm=512):
    b, s, d = h.shape
    k = o.shape[2]
    return pl.pallas_call(
        _outproj_kernel,
        grid=(b, s // tm),
        in_specs=[pl.BlockSpec((1, tm, k), lambda bi, i: (bi, i, 0)),
                  pl.BlockSpec((k, d), lambda bi, i: (0, 0)),
                  pl.BlockSpec((1, 1, d), lambda bi, i: (bi, 0, 0)),
                  pl.BlockSpec((1, tm, d), lambda bi, i: (bi, i, 0))],
        out_specs=pl.BlockSpec((1, tm, d), lambda bi, i: (bi, i, 0)),
        out_shape=jax.ShapeDtypeStruct((b, s, d), F32),
        compiler_params=_cparams("arbitrary", "arbitrary"),
        name="outproj_residual",
    )(o, w, gate, h)


def _ffn_kernel(h_ref, g_ref, sh_ref, sc_ref, gate_ref, wg_ref, wu_ref, wd_ref, out_ref, u_scr, acc_scr):
    f = pl.program_id(2)

    @pl.when(f == 0)
    def _():
        u_scr[...] = _norm_mod(h_ref[0], g_ref[...], sh_ref[0], sc_ref[0]).astype(BF16)
        acc_scr[...] = jnp.zeros(acc_scr.shape, F32)

    ub = u_scr[...]
    a = jnp.dot(ub, wg_ref[...], preferred_element_type=F32)
    up = jnp.dot(ub, wu_ref[...], preferred_element_type=F32)
    hid = (a * jax.nn.sigmoid(a)) * up
    acc_scr[...] += jnp.dot(hid.astype(BF16), wd_ref[...], preferred_element_type=F32)

    @pl.when(f == pl.num_programs(2) - 1)
    def _():
        out_ref[0] = h_ref[0] + gate_ref[0] * acc_scr[...]


def _ffn_residual(h, gain, shift, scale, gate, wg, wu, wd, tm=512):
    b, s, d = h.shape
    dff = wg.shape[1]
    tf = dff
    vec = pl.BlockSpec((1, 1, d), lambda bi, i, f: (bi, 0, 0))
    once = pl.Buffered(1)
    return pl.pallas_call(
        _ffn_kernel,
        grid=(b, s // tm, dff // tf),
        in_specs=[pl.BlockSpec((1, tm, d), lambda bi, i, f: (bi, i, 0)),
                  pl.BlockSpec((1, d), lambda bi, i, f: (0, 0)),
                  vec, vec, vec,
                  pl.BlockSpec((d, tf), lambda bi, i, f: (0, f), pipeline_mode=once),
                  pl.BlockSpec((d, tf), lambda bi, i, f: (0, f), pipeline_mode=once),
                  pl.BlockSpec((tf, d), lambda bi, i, f: (f, 0), pipeline_mode=once)],
        out_specs=pl.BlockSpec((1, tm, d), lambda bi, i, f: (bi, i, 0)),
        out_shape=jax.ShapeDtypeStruct((b, s, d), F32),
        scratch_shapes=[pltpu.VMEM((tm, d), BF16), pltpu.VMEM((tm, d), F32)],
        compiler_params=_cparams("arbitrary", "arbitrary", "arbitrary"),
        name="ffn_residual",
    )(h, gain.reshape(1, d), shift, scale, gate, wg, wu, wd)


def _compress_kernel(x_ref, pe_ref, w1_ref, w2_ref, o_ref):
    x = x_ref[0, 0, 0]
    nr = x.shape[0]
    pa = jnp.dot((x + pe_ref[0, 0]).astype(BF16), w1_ref[0, 0], preferred_element_type=F32)
    pb = jnp.dot((x + pe_ref[0, 1]).astype(BF16), w1_ref[0, 1], preferred_element_type=F32)
    hid = pa + pltpu.roll(pb, nr - 1, 0)
    act = 0.5 * hid * (1.0 + jnp.tanh(np.float32(np.sqrt(2.0 / np.pi)) * (hid + 0.044715 * (hid * hid * hid))))
    o_ref[0, 0, 0] = jnp.dot(act.astype(BF16), w2_ref[0], preferred_element_type=F32).astype(BF16)


def _compress(x, pe, w1, w2):
    two, b, g, nr, kk = x.shape
    return pl.pallas_call(
        _compress_kernel,
        grid=(two, b, g),
        in_specs=[pl.BlockSpec((1, 1, 1, nr, kk), lambda a, bi, gi: (a, bi, gi, 0, 0)),
                  pl.BlockSpec((1, 2, 1, kk), lambda a, bi, gi: (a, 0, 0, 0)),
                  pl.BlockSpec((1, 2, kk, CMP_HIDDEN), lambda a, bi, gi: (a, 0, 0, 0)),
                  pl.BlockSpec((1, CMP_HIDDEN, HEAD_DIM), lambda a, bi, gi: (a, 0, 0))],
        out_specs=pl.BlockSpec((1, 1, 1, nr, HEAD_DIM), lambda a, bi, gi: (a, bi, gi, 0, 0)),
        out_shape=jax.ShapeDtypeStruct((two, b, g, nr, HEAD_DIM), BF16),
        compiler_params=_cparams("arbitrary", "arbitrary", "arbitrary"),
        name="nsa_compress",
    )(x, pe, w1, w2)


def _nsa_kernel(q_ref, gt_ref, kc_ref, vc_ref, ks_ref, kw_ref, vs_ref, vw_ref, aggt_ref, o_ref,
                qs_scr, oc_scr, m_scr, acc_scr, *, n_sel):
    i = pl.program_id(1)
    q0 = i * TQ
    nch = (q0 + TQ + TK - 1) // TK
    _stack_heads(q_ref, qs_scr)
    _init_online(m_scr, acc_scr)
    t = q0 + lax.broadcasted_iota(jnp.int32, (TQ, 1), 0)
    t4 = jnp.concatenate([t] * GROUP, axis=0)

    n_cmp = kc_ref.shape[1]
    cmp_end = lax.broadcasted_iota(jnp.int32, (1, n_cmp), 1) * CMP_STRIDE + (CMP_LEN - 1)
    cmask = cmp_end <= t4
    n_blk = aggt_ref.shape[0]
    jrow = lax.broadcasted_iota(jnp.int32, (n_blk, TQ), 0)
    tq = q0 + lax.broadcasted_iota(jnp.int32, (n_blk, TQ), 1)
    jt = tq >> SLC_SHIFT
    valid = jrow * SLC_LEN <= tq
    forced = (jrow == 0) | (jrow == jt) | (jrow == jt - 1)
    for g in range(N_KV_HEADS):
        sc = lax.dot_general(qs_scr[g], kc_ref[0, :, g * LANE:(g + 1) * LANE], _NT, preferred_element_type=F32)
        sm = jnp.where(cmask, sc, NEG_INF)
        e = jnp.where(cmask, jnp.exp2(sm - jnp.max(sm, axis=1, keepdims=True)), 0.0)
        pc = e / jnp.maximum(jnp.sum(e, axis=1, keepdims=True), 1e-30)
        oc_scr[g] = jnp.dot(pc.astype(BF16), vc_ref[0, :, _head(g)], preferred_element_type=F32)
        pcs = pc[0:TQ]
        for r in range(1, GROUP):
            pcs = pcs + pc[r * TQ:(r + 1) * TQ]
        imp = lax.dot_general(aggt_ref[...], pcs, _NT, preferred_element_type=F32,
                              precision=lax.Precision.HIGHEST)
        imp = jnp.where(valid, jnp.where(forced, POS_INF, imp), NEG_INF)
        rank = jnp.zeros((n_blk, TQ), jnp.int32)
        for j in range(n_blk):
            row = imp[j:j + 1, :]
            beats = (row > imp) | ((row == imp) & (jrow > j))
            rank = rank + beats.astype(jnp.int32)
        sel = (rank < n_sel) & valid
        selb = jnp.where(sel, 0.0, NEG_INF)
        pads = [jnp.zeros((HEAD_DIM, TQ), F32), selb]
        if n_blk < LANE - HEAD_DIM:
            pads.append(jnp.zeros((LANE - HEAD_DIM - n_blk, TQ), F32))
        selb = jnp.concatenate(pads, axis=0)
        selb = selb.T.astype(BF16)
        qs_scr[g] = qs_scr[g] + jnp.concatenate([selb] * GROUP, axis=0)

    def slc_chunk(c, diagonal):
        k0 = pl.multiple_of(c * TK, TK)
        for g in range(N_KV_HEADS):
            s = lax.dot_general(qs_scr[g], ks_ref[0, pl.ds(k0, TK), _head(g)], _NT,
                                preferred_element_type=F32)
            if diagonal:
                pos = k0 + lax.broadcasted_iota(jnp.int32, (1, TK), 1)
                s = jnp.where(pos <= t4, s, NEG_INF)
            _online_update(s, vs_ref[0, pl.ds(k0, TK), _head(g)], m_scr, acc_scr, g)

    def slc_body(c, carry):
        slc_chunk(c, False)
        return carry

    lax.fori_loop(0, nch - 1, slc_body, 0)
    slc_chunk(nch - 1, True)

    gt = gt_ref[0]
    w0 = pl.multiple_of(jnp.maximum(q0 - WINDOW, 0), TQ)
    wpos = w0 + lax.broadcasted_iota(jnp.int32, (1, WIN_SPAN), 1)
    wmask = (wpos <= t4) & (wpos > t4 - WINDOW)
    for g in range(N_KV_HEADS):
        s = lax.dot_general(qs_scr[g], kw_ref[0, pl.ds(w0, WIN_SPAN), _head(g)], _NT,
                            preferred_element_type=F32)
        s = jnp.where(wmask, s, NEG_INF)
        p = jnp.exp2(s - jnp.max(s, axis=1, keepdims=True))
        p = jnp.where(wmask, p, 0.0)
        ow = _normalised(jnp.dot(p.astype(BF16), vw_ref[0, pl.ds(w0, WIN_SPAN), _head(g)],
                                 preferred_element_type=F32))
        osl = _normalised(acc_scr[g])
        oc = oc_scr[g]
        for r in range(GROUP):
            hh = g * GROUP + r
            rs = slice(r * TQ, (r + 1) * TQ)
            o = (gt[:, hh:hh + 1] * oc[rs] + gt[:, N_HEADS + hh:N_HEADS + hh + 1] * osl[rs]
                 + gt[:, 2 * N_HEADS + hh:2 * N_HEADS + hh + 1] * ow[rs])
            o_ref[0, :, hh * LANE:(hh + 1) * LANE] = o.astype(BF16)


def _nsa_attention(q, gates, kc, vc, kv, agg_t):
    b, s_len, _ = q.shape
    n_cmp = kc.shape[1]
    n_blk = agg_t.shape[0]
    n_sel = min(SLC_TOPN, s_len // SLC_LEN)
    rows = GROUP * TQ
    return pl.pallas_call(
        functools.partial(_nsa_kernel, n_sel=n_sel),
        grid=(b, s_len // TQ),
        in_specs=[pl.BlockSpec((1, TQ, N_HEADS * LANE), lambda bi, i: (bi, i, 0)),
                  pl.BlockSpec((1, TQ, LANE), lambda bi, i: (bi, i, 0)),
                  pl.BlockSpec((1, n_cmp, N_KV_HEADS * LANE), lambda bi, i: (bi, 0, 0)),
                  pl.BlockSpec((1, n_cmp, N_KV_HEADS * LANE), lambda bi, i: (bi, 0, 0)),
                  pl.BlockSpec((1, s_len, N_KV_HEADS * LANE), lambda bi, i: (bi, 0, 0), pipeline_mode=pl.Buffered(1)),
                  pl.BlockSpec((1, s_len, N_KV_HEADS * LANE), lambda bi, i: (bi, 0, 1), pipeline_mode=pl.Buffered(1)),
                  pl.BlockSpec((1, s_len, N_KV_HEADS * LANE), lambda bi, i: (bi, 0, 2), pipeline_mode=pl.Buffered(1)),
                  pl.BlockSpec((1, s_len, N_KV_HEADS * LANE), lambda bi, i: (bi, 0, 3), pipeline_mode=pl.Buffered(1)),
                  pl.BlockSpec((n_blk, n_cmp), lambda bi, i: (0, 0))],
        out_specs=pl.BlockSpec((1, TQ, N_HEADS * LANE), lambda bi, i: (bi, i, 0)),
        out_shape=jax.ShapeDtypeStruct((b, s_len, N_HEADS * LANE), BF16),
        scratch_shapes=[pltpu.VMEM((N_KV_HEADS, rows, LANE), BF16),
                        pltpu.VMEM((N_KV_HEADS, rows, LANE), F32),
                        pltpu.VMEM((N_KV_HEADS, rows, LANE), F32),
                        pltpu.VMEM((N_KV_HEADS, rows, LANE), F32)],
        compiler_params=_cparams("arbitrary", "arbitrary"),
        name="nsa_attention",
    )(q, gates, kc, vc, kv, kv, kv, kv, agg_t)


def _router_kernel(h_ref, g_ref, sh_ref, sc_ref, wr_ref, low_ref, u_ref, info_ref, cnt_ref, carry_scr):
    @pl.when((pl.program_id(0) == 0) & (pl.program_id(1) == 0))
    def _():
        carry_scr[...] = jnp.zeros(carry_scr.shape, F32)

    u = _norm_mod(h_ref[0], g_ref[...], sh_ref[0], sc_ref[0])
    u_ref[0] = u.astype(BF16)
    logits = jnp.dot(u, wr_ref[...], preferred_element_type=F32, precision=lax.Precision.HIGHEST)
    lane = lax.broadcasted_iota(jnp.int32, logits.shape, 1)
    logits = jnp.where(lane < N_EXPERTS, logits, -jnp.inf)
    m1 = jnp.max(logits, axis=1, keepdims=True)
    i1 = jnp.min(jnp.where(logits == m1, lane, LANE), axis=1, keepdims=True)
    rest = jnp.where(lane == i1, -jnp.inf, logits)
    m2 = jnp.max(rest, axis=1, keepdims=True)
    i2 = jnp.min(jnp.where(rest == m2, lane, LANE), axis=1, keepdims=True)
    e2 = jnp.exp(m2 - m1)
    w1 = 1.0 / (1.0 + e2)
    w2 = e2 / (1.0 + e2)
    hit1 = lane == i1
    hit2 = lane == i2
    onehot = jnp.where(hit1 | hit2, 1.0, 0.0).astype(BF16)
    ahead = jnp.dot(low_ref[...], onehot, preferred_element_type=F32) + carry_scr[0:1, :]
    pos1 = jnp.sum(jnp.where(hit1, ahead, 0.0), axis=1, keepdims=True)
    pos2 = jnp.sum(jnp.where(hit2, ahead, 0.0), axis=1, keepdims=True)
    tile_cnt = jnp.dot(jnp.ones((8, TMX), BF16), onehot, preferred_element_type=F32)
    cnt_ref[0] = tile_cnt
    carry_scr[...] += tile_cnt
    info = jnp.zeros(logits.shape, F32)
    for j, v in enumerate((i1.astype(F32), i2.astype(F32), pos1, pos2, w1, w2)):
        info = jnp.where(lane == j, v, info)
    info_ref[0] = info


def _moe_gather_kernel(gd_ref, gs_ref, nv_ref, u_ref, dr_ref, w_ref, xs_ref, ws_ref):
    k = pl.program_id(0)
    d = gd_ref[k]

    @pl.when((k == 0) | (gd_ref[jnp.maximum(k - 1, 0)] != d))
    def _():
        xs_ref[...] = jnp.zeros(xs_ref.shape, BF16)
        ws_ref[...] = jnp.zeros(ws_ref.shape, F32)

    @pl.when(k < nv_ref[0])
    def _():
        row = d * RBLK + lax.broadcasted_iota(jnp.int32, (RBLK, TMX), 0)
        p1 = dr_ref[0, 0:1, :] == row
        p2 = dr_ref[0, 1:2, :] == row
        onehot = jnp.where(p1 | p2, 1.0, 0.0).astype(BF16)
        xs_ref[...] += jnp.dot(onehot, u_ref[...], preferred_element_type=F32).astype(BF16)
        wrow = jnp.where(p1, w_ref[0, 0:1, :], 0.0) + jnp.where(p2, w_ref[0, 1:2, :], 0.0)
        ws_ref[...] += jnp.sum(wrow, axis=1, keepdims=True)


def _moe_ffn_kernel(te_ref, nt_ref, x_ref, ws_ref, wg_ref, wu_ref, wd_ref, o_ref, acc_scr):
    i = pl.program_id(0)
    f = pl.program_id(1)
    live = i < nt_ref[0]

    @pl.when(f == 0)
    def _():
        acc_scr[...] = jnp.zeros(acc_scr.shape, F32)

    @pl.when(live)
    def _():
        x = x_ref[...]
        a = jnp.dot(x, wg_ref[0], preferred_element_type=F32)
        up = jnp.dot(x, wu_ref[0], preferred_element_type=F32)
        hid = (a * jax.nn.sigmoid(a)) * up
        acc_scr[...] += jnp.dot(hid.astype(BF16), wd_ref[0], preferred_element_type=F32)

    @pl.when(f == pl.num_programs(1) - 1)
    def _():
        w = jnp.concatenate([ws_ref[...]] * (acc_scr.shape[1] // LANE), axis=1)
        o_ref[...] = jnp.where(live, acc_scr[...] * w, 0.0).astype(BF16)


def _moe_combine_kernel(ys_ref, yd_ref, nv_ref, o_ref, dr_ref, h_ref, gate_ref, fg_ref, out_ref, y_scr):
    k = pl.program_id(0)
    last_k = pl.num_programs(0) - 1
    s = ys_ref[k]
    d = yd_ref[k]

    @pl.when((k == 0) | (ys_ref[jnp.maximum(k - 1, 0)] != s))
    def _():
        y_scr[...] = jnp.zeros(y_scr.shape, F32)

    @pl.when(k < nv_ref[0])
    def _():
        col = d * RBLK + lax.broadcasted_iota(jnp.int32, (TMX, RBLK), 1)
        dr = dr_ref[...]
        onehot = jnp.where((dr[:, 0:1] == col) | (dr[:, 1:2] == col), 1.0, 0.0).astype(BF16)
        y_scr[...] += jnp.dot(onehot, o_ref[...], preferred_element_type=F32)

    @pl.when((k == last_k) | (ys_ref[jnp.minimum(k + 1, last_k)] != s))
    def _():
        hn = h_ref[0] + gate_ref[0] * y_scr[...]
        ms = jnp.mean(hn * hn, axis=-1, keepdims=True)
        out_ref[0] = (hn * lax.rsqrt(ms + RMS_EPS)) * fg_ref[...]


def _moe_plan(info, cnt, n_tok):
    nt = n_tok // TMX
    n_blk = (TOP_K_EXPERTS * n_tok + N_EXPERTS * GRP) // RBLK
    n_pairs = n_blk + N_EXPERTS * nt
    e1, e2, pos1, pos2 = [info[:, j].astype(jnp.int32) for j in range(4)]
    counts = cnt[:, 0, :N_EXPERTS].astype(jnp.int32)
    padded = (jnp.sum(counts, axis=0) + GRP - 1) // GRP * GRP
    ends = jnp.cumsum(padded)
    start = ends - padded
    dr1 = start[e1] + pos1
    dr2 = start[e2] + pos2
    first = start[None, :] + jnp.cumsum(counts, axis=0) - counts
    last = first + counts - 1
    fb = first // RBLK
    nb = jnp.where(counts > 0, last // RBLK - fb + 1, 0)
    k3 = jnp.arange(3)
    cd = (fb[..., None] + k3).reshape(-1)
    cs = jnp.broadcast_to(jnp.arange(nt)[:, None, None], (nt, N_EXPERTS, 3)).reshape(-1)
    cv = (k3 < nb[..., None]).reshape(-1)
    n_valid = jnp.sum(cv.astype(jnp.int32))

    def ordered(key):
        smaller = cv[None, :] & (key[None, :] < key[:, None])
        slot = jnp.where(cv, jnp.sum(smaller.astype(jnp.int32), axis=1), -1)
        want = jnp.minimum(jnp.arange(n_pairs), n_valid - 1)
        pick = slot[None, :] == want[:, None]
        return (jnp.sum(jnp.where(pick, cd[None, :], 0), axis=1).astype(jnp.int32),
                jnp.sum(jnp.where(pick, cs[None, :], 0), axis=1).astype(jnp.int32))

    gd, gs = ordered(cd * nt + cs)
    yd, ys = ordered(cs * n_blk + cd)
    n_ffn = n_blk * RBLK // GRP
    tile_row = jnp.arange(n_ffn) * GRP
    tile_e = jnp.minimum(jnp.sum((ends[None, :] <= tile_row[:, None]).astype(jnp.int32), axis=1), N_EXPERTS - 1)
    return dict(dr1=dr1, dr2=dr2, gd=gd, gs=gs, yd=yd, ys=ys, n_valid=n_valid.reshape(1),
                tile_e=tile_e.astype(jnp.int32), n_live=(ends[-1] // GRP).astype(jnp.int32).reshape(1),
                n_blk=n_blk, n_pairs=n_pairs, n_ffn=n_ffn)


def _moe_residual_norm(h, gain, shift, scale, gate, w_router, final_gain, wg, wu, wd, tf=1792):
    b, s, d = h.shape
    n_tok = b * s
    tpb = s // TMX
    nt = n_tok // TMX
    ne, _, dff = wg.shape
    vec = pl.BlockSpec((1, 1, d), lambda bi, i: (bi, 0, 0))
    low = jnp.asarray(np.tril(np.ones((TMX, TMX), np.float32), -1), BF16)
    u, info, cnt = pl.pallas_call(
        _router_kernel,
        grid=(b, tpb),
        in_specs=[pl.BlockSpec((1, TMX, d), lambda bi, i: (bi, i, 0)),
                  pl.BlockSpec((1, d), lambda bi, i: (0, 0)),
                  vec, vec,
                  pl.BlockSpec((d, LANE), lambda bi, i: (0, 0)),
                  pl.BlockSpec((TMX, TMX), lambda bi, i: (0, 0))],
        out_specs=[pl.BlockSpec((1, TMX, d), lambda bi, i: (bi, i, 0)),
                   pl.BlockSpec((1, TMX, LANE), lambda bi, i: (bi, i, 0)),
                   pl.BlockSpec((1, 8, LANE), lambda bi, i: (bi * tpb + i, 0, 0))],
        out_shape=[jax.ShapeDtypeStruct((b, s, d), BF16),
                   jax.ShapeDtypeStruct((b, s, LANE), F32),
                   jax.ShapeDtypeStruct((nt, 8, LANE), F32)],
        scratch_shapes=[pltpu.VMEM((8, LANE), F32)],
        compiler_params=_cparams("arbitrary", "arbitrary"),
        name="moe_router",
    )(h, gain.reshape(1, d), shift, scale, w_router, low)

    info = info.reshape(n_tok, LANE)
    plan = _moe_plan(info, cnt, n_tok)
    n_blk, n_pairs, n_ffn = plan["n_blk"], plan["n_pairs"], plan["n_ffn"]
    pad6 = jnp.zeros((nt, 6, TMX), jnp.int32)
    dr_rows = jnp.concatenate([plan["dr1"].reshape(nt, 1, TMX), plan["dr2"].reshape(nt, 1, TMX), pad6], axis=1)
    w_rows = jnp.concatenate([info[:, 4].reshape(nt, 1, TMX), info[:, 5].reshape(nt, 1, TMX),
                              pad6.astype(F32)], axis=1)
    dr_cols = jnp.pad(jnp.stack([plan["dr1"], plan["dr2"]], axis=1), ((0, 0), (0, LANE - 2)))

    xs, ws = pl.pallas_call(
        _moe_gather_kernel,
        grid_spec=pltpu.PrefetchScalarGridSpec(
            num_scalar_prefetch=3,
            grid=(n_pairs,),
            in_specs=[pl.BlockSpec((TMX, d), lambda k, gd, gs, nv: (gs[k], 0)),
                      pl.BlockSpec((1, 8, TMX), lambda k, gd, gs, nv: (gs[k], 0, 0)),
                      pl.BlockSpec((1, 8, TMX), lambda k, gd, gs, nv: (gs[k], 0, 0))],
            out_specs=[pl.BlockSpec((RBLK, d), lambda k, gd, gs, nv: (gd[k], 0)),
                       pl.BlockSpec((RBLK, LANE), lambda k, gd, gs, nv: (gd[k], 0))]),
        out_shape=[jax.ShapeDtypeStruct((n_blk * RBLK, d), BF16),
                   jax.ShapeDtypeStruct((n_blk * RBLK, LANE), F32)],
        compiler_params=_cparams("arbitrary"),
        name="moe_gather",
    )(plan["gd"], plan["gs"], plan["n_valid"], u.reshape(n_tok, d), dr_rows, w_rows)

    def live_tile(i, te, nl):
        return jnp.maximum(jnp.minimum(i, nl[0] - 1), 0)

    rows_out = pl.pallas_call(
        _moe_ffn_kernel,
        grid_spec=pltpu.PrefetchScalarGridSpec(
            num_scalar_prefetch=2,
            grid=(n_ffn, dff // tf),
            in_specs=[pl.BlockSpec((GRP, d), lambda i, f, te, nl: (live_tile(i, te, nl), 0)),
                      pl.BlockSpec((GRP, LANE), lambda i, f, te, nl: (live_tile(i, te, nl), 0)),
                      pl.BlockSpec((1, d, tf), lambda i, f, te, nl: (te[i], 0, f)),
                      pl.BlockSpec((1, d, tf), lambda i, f, te, nl: (te[i], 0, f)),
                      pl.BlockSpec((1, tf, d), lambda i, f, te, nl: (te[i], f, 0))],
            out_specs=pl.BlockSpec((GRP, d), lambda i, f, te, nl: (i, 0)),
            scratch_shapes=[pltpu.VMEM((GRP, d), F32)]),
        out_shape=jax.ShapeDtypeStruct((n_blk * RBLK, d), BF16),
        compiler_params=_cparams("arbitrary", "arbitrary"),
        name="moe_experts",
    )(plan["tile_e"], plan["n_live"], xs, ws, wg, wu, wd)

    return pl.pallas_call(
        _moe_combine_kernel,
        grid_spec=pltpu.PrefetchScalarGridSpec(
            num_scalar_prefetch=3,
            grid=(n_pairs,),
            in_specs=[pl.BlockSpec((RBLK, d), lambda k, ys, yd, nv: (yd[k], 0)),
                      pl.BlockSpec((TMX, LANE), lambda k, ys, yd, nv: (ys[k], 0)),
                      pl.BlockSpec((1, TMX, d), lambda k, ys, yd, nv: (ys[k] // tpb, ys[k] % tpb, 0)),
                      pl.BlockSpec((1, 1, d), lambda k, ys, yd, nv: (ys[k] // tpb, 0, 0)),
                      pl.BlockSpec((1, d), lambda k, ys, yd, nv: (0, 0))],
            out_specs=pl.BlockSpec((1, TMX, d), lambda k, ys, yd, nv: (ys[k] // tpb, ys[k] % tpb, 0)),
            scratch_shapes=[pltpu.VMEM((TMX, d), F32)]),
        out_shape=jax.ShapeDtypeStruct((b, s, d), F32),
        compiler_params=_cparams("arbitrary"),
        name="moe_combine_norm",
    )(plan["ys"], plan["yd"], plan["n_valid"], rows_out, dr_cols, h, gate, final_gain.reshape(1, d))


def _pad_heads(w, n_heads):
    d = w.shape[0]
    w = w.reshape(d, n_heads, HEAD_DIM)
    return jnp.pad(w, ((0, 0), (0, 0), (0, LANE - HEAD_DIM))).reshape(d, n_heads * LANE)


def _pad_cols(w, n):
    return jnp.pad(w, ((0, 0), (0, n - w.shape[1])))


def _out_rows(w_out):
    d = w_out.shape[1]
    w = w_out.reshape(N_HEADS, HEAD_DIM, d)
    return jnp.pad(w, ((0, 0), (0, LANE - HEAD_DIM), (0, 0))).reshape(N_HEADS * LANE, d)


def _dsa_weights(w_in):
    q, k, v, iq, ik, iw = jnp.split(w_in, [1024, 1280, 1536, 2048, 2112], axis=1)
    w = jnp.concatenate([_pad_heads(q, N_HEADS), _pad_heads(iq, IDX_HEADS), _pad_heads(k, N_KV_HEADS),
                         _pad_heads(v, N_KV_HEADS), _pad_cols(ik, LANE), _pad_cols(iw, LANE)], axis=1).astype(BF16)
    plan = [(0, 2048, True, Q_SCALE, 0, 0, None),
            (2048, 1024, True, 1.0, 0, 2048, None),
            (3072, 512, True, 1.0, 0, 3072, None),
            (3584, 512, False, 1.0, 0, 3584, "ones"),
            (4096, 128, True, 1.0, 0, 4096, None),
            (4224, 128, False, IDX_HEADS ** -0.5 * IDX_DIM ** -0.5, 1, 0, None)]
    return w, plan, [(4224, BF16), (LANE, F32)]


def _nsa_q_weights(w_q):
    q = w_q[:, :N_HEADS * HEAD_DIM]
    gates = w_q[:, N_HEADS * HEAD_DIM:].reshape(-1, N_HEADS, N_NSA_BRANCH)
    gates = jnp.transpose(gates, (0, 2, 1)).reshape(-1, N_NSA_BRANCH * N_HEADS)
    w = jnp.concatenate([_pad_heads(q, N_HEADS), _pad_cols(gates, LANE)], axis=1).astype(BF16)
    plan = [(0, 2048, True, Q_SCALE, 0, 0, None),
            (2048, 128, False, 1.0, 1, 0, "sigmoid")]
    return w, plan, [(2048, BF16), (LANE, F32)]


def _kv_weights(w_kv):
    kvw = w_kv.reshape(-1, 2 * N_NSA_BRANCH, N_KV_HEADS * HEAD_DIM)
    k_cmp, v_cmp, k_slc, v_slc, k_win, v_win = [kvw[:, j] for j in range(2 * N_NSA_BRANCH)]
    w = jnp.concatenate([_pad_heads(k_slc, N_KV_HEADS), _pad_heads(k_win, N_KV_HEADS),
                         _pad_heads(v_slc, N_KV_HEADS), _pad_heads(v_win, N_KV_HEADS),
                         k_cmp, v_cmp], axis=1).astype(BF16)
    plan = [(0, 512, True, 1.0, 0, 0, "blockhot"),
            (512, 512, True, 1.0, 0, 512, None),
            (1024, 512, False, 1.0, 0, 1024, "ones"),
            (1536, 512, False, 1.0, 0, 1536, "ones"),
            (2048, 256, True, 1.0, 1, 0, None),
            (2304, 256, False, 1.0, 1, 256, None)]
    return w, plan, [(2048, BF16), (512, F32)]


def _nsa_constants(s_len):
    n_cmp = (s_len - CMP_LEN) // CMP_STRIDE + 1
    n_slc = s_len // SLC_LEN
    cmp_start = np.arange(n_cmp) * CMP_STRIDE
    slc_start = np.arange(n_slc) * SLC_LEN
    ov = (np.minimum(cmp_start[:, None] + CMP_LEN, slc_start[None, :] + SLC_LEN)
          - np.maximum(cmp_start[:, None], slc_start[None, :]))
    agg = (np.clip(ov, 0, None) / CMP_LEN).astype(np.float32)
    agg_t = np.zeros((n_slc, s_len // CMP_STRIDE), np.float32)
    agg_t[:, :n_cmp] = agg.T
    return jnp.asarray(agg_t)


def kernel(x, c, positions, attn_gain, ffn_gain, w_ada, b_ada, a_w_in, a_w_out, b_w_q, b_w_out, kv_gain, w_kv_ada, b_kv_ada, w_kv, cmp_pe_k, cmp_w1_k, cmp_w2_k, cmp_pe_v, cmp_w1_v, cmp_w2_v, ffn_w_gate, ffn_w_up, ffn_w_down, moe_w_router, moe_w_gate, moe_w_up, moe_w_down, final_gain):
    b, s_len, d = x.shape
    assert s_len % TK == 0 and s_len // SLC_LEN <= HEAD_DIM and s_len >= WIN_SPAN and b <= 8

    cos_t, sin_t = _rope_tables(positions)
    c_pad = jnp.pad(c, ((0, 8 - b), (0, 0)))
    mod = _ada(c_pad, w_ada, b_ada)[:, :b]
    kv_mod = _ada(c_pad, w_kv_ada[None], b_kv_ada[None])[0, :b]

    def parts(m, n):
        return [p[:, None, :] for p in jnp.split(m, n, axis=-1)]

    a_shift, a_scale, a_gate, f_shift, f_scale, f_gate = parts(mod[0], 6)
    w, plan, outs = _dsa_weights(a_w_in[0])
    proj, iw = _norm_proj(x, attn_gain[0], a_shift, a_scale, cos_t, sin_t, w, plan, outs)
    bias = _dsa_bias(proj, iw, s_len)
    o = _dsa_attention(proj, bias)
    h = _outproj_residual(o, _out_rows(a_w_out[0]).astype(BF16), a_gate, x)
    h = _ffn_residual(h, ffn_gain[0], f_shift, f_scale, f_gate,
                      ffn_w_gate[0].astype(BF16), ffn_w_up[0].astype(BF16), ffn_w_down[0].astype(BF16))

    kv_shift, kv_scale = parts(kv_mod, 2)
    w, plan, outs = _kv_weights(w_kv)
    kv, kv_cmp = _norm_proj(h, kv_gain, kv_shift, kv_scale, cos_t, sin_t, w, plan, outs)
    nr = s_len // CMP_STRIDE
    xc = kv_cmp.reshape(b, s_len, 2, N_KV_HEADS, HEAD_DIM)
    xc = jnp.transpose(xc, (2, 0, 3, 1, 4)).reshape(2, b, N_KV_HEADS, nr, CMP_STRIDE * HEAD_DIM)
    pe = jnp.stack([cmp_pe_k, cmp_pe_v]).reshape(2, 2, 1, CMP_STRIDE * HEAD_DIM)
    w1 = jnp.stack([cmp_w1_k, cmp_w1_v]).reshape(2, 2, CMP_STRIDE * HEAD_DIM, CMP_HIDDEN).astype(BF16)
    w2 = jnp.stack([cmp_w2_k, cmp_w2_v]).astype(BF16)
    cmp = _compress(xc, pe, w1, w2)
    cmp = jnp.transpose(cmp, (0, 1, 3, 2, 4))
    cmp = jnp.pad(cmp, ((0, 0), (0, 0), (0, 0), (0, 0), (0, LANE - HEAD_DIM)))
    kc, vc = cmp.reshape(2, b, nr, N_KV_HEADS * LANE)

    a_shift, a_scale, a_gate, f_shift, f_scale, f_gate = parts(mod[1], 6)
    w, plan, outs = _nsa_q_weights(b_w_q[0])
    q, gates = _norm_proj(h, attn_gain[1], a_shift, a_scale, cos_t, sin_t, w, plan, outs)
    o = _nsa_attention(q, gates, kc, vc, kv, _nsa_constants(s_len))
    h = _outproj_residual(o, _out_rows(b_w_out[0]).astype(BF16), a_gate, h)
    return _moe_residual_norm(h, ffn_gain[1], f_shift, f_scale, f_gate, _pad_cols(moe_w_router[0], LANE),
                              final_gain, moe_w_gate[0].astype(BF16), moe_w_up[0].astype(BF16),
                              moe_w_down[0].astype(BF16))
```

```python
import functools

import numpy as np
import jax
import jax.numpy as jnp
from jax import lax
from jax.experimental import pallas as pl
from jax.experimental.pallas import tpu as pltpu

F32 = jnp.float32
BF16 = jnp.bfloat16

D_MODEL = 1024
HEAD_DIM = 64
N_HEADS = 16
N_KV_HEADS = 4
GROUP = N_HEADS // N_KV_HEADS
ROPE_THETA = 10000.0
RMS_EPS = 1e-6
NEG_INF = -1e30
MASKED = float("-inf")
POS_INF = 1e30
IDX_HEADS = 8
IDX_DIM = HEAD_DIM
DSA_TOPK = 256
CMP_LEN = 32
CMP_STRIDE = 16
CMP_HIDDEN = 256
SLC_LEN = 64
SLC_SHIFT = 6
SLC_TOPN = 16
WINDOW = 512
N_NSA_BRANCH = 3
N_EXPERTS = 8
TOP_K_EXPERTS = 2

LANE = 128
TQ = 256
TQI = 256
RB = 64
TK = 512
WIN_SPAN = WINDOW + TQ
TMX = 512
RBLK = 256
GRP = 512
INT_MIN = np.int32(-2 ** 31)
M_INIT = -3.0e38
Q_SCALE = HEAD_DIM ** -0.5 * float(np.log2(np.e))
VMEM_LIMIT = 56 * 1024 * 1024

_NT = (((1,), (1,)), ((), ()))


def _cparams(*sem, flags=None):
    return pltpu.CompilerParams(dimension_semantics=sem, vmem_limit_bytes=VMEM_LIMIT, flags=flags)


def _rope_kernel(pos_ref, inv_ref, sgn_ref, cos_ref, sin_ref):
    ang = pos_ref[0].astype(F32) * inv_ref[...]
    cos_ref[0] = jnp.cos(ang)
    sin_ref[0] = jnp.sin(ang) * sgn_ref[...]


def _rope_tables(positions):
    b, s = positions.shape
    inv = 1.0 / (ROPE_THETA ** (jnp.arange(0, HEAD_DIM, 2, dtype=F32) / HEAD_DIM))
    inv = jnp.tile(inv, LANE // (HEAD_DIM // 2))[None]
    sgn = np.where((np.arange(LANE) % HEAD_DIM) < HEAD_DIM // 2, -1.0, 1.0).astype(np.float32)[None]
    return pl.pallas_call(
        _rope_kernel,
        grid=(b,),
        in_specs=[pl.BlockSpec((1, s, 1), lambda i: (i, 0, 0)),
                  pl.BlockSpec((1, LANE), lambda i: (0, 0)),
                  pl.BlockSpec((1, LANE), lambda i: (0, 0))],
        out_specs=[pl.BlockSpec((1, s, LANE), lambda i: (i, 0, 0))] * 2,
        out_shape=[jax.ShapeDtypeStruct((b, s, LANE), F32)] * 2,
        compiler_params=_cparams("arbitrary"),
        name="rope_tables",
    )(positions[..., None], inv, jnp.asarray(sgn))


def _ada_kernel(c_ref, w_ref, b_ref, o_ref):
    c = c_ref[...]
    ca = c * jax.nn.sigmoid(c)
    o_ref[0] = jnp.dot(ca, w_ref[0], preferred_element_type=F32,
                       precision=lax.Precision.HIGHEST) + b_ref[0]


def _ada(c_pad, w, bias):
    nl, d, n = w.shape
    tn = 1024
    return pl.pallas_call(
        _ada_kernel,
        grid=(nl, n // tn),
        in_specs=[pl.BlockSpec((8, d), lambda l, j: (0, 0)),
                  pl.BlockSpec((1, d, tn), lambda l, j: (l, 0, j)),
                  pl.BlockSpec((1, 1, tn), lambda l, j: (l, 0, j))],
        out_specs=pl.BlockSpec((1, 8, tn), lambda l, j: (l, 0, j)),
        out_shape=jax.ShapeDtypeStruct((nl, 8, n), F32),
        compiler_params=_cparams("arbitrary", "arbitrary"),
        name="ada_mod",
    )(c_pad, w, bias.reshape(nl, 1, n))


def _norm_mod(x, g, shift, scale):
    ms = jnp.mean(x * x, axis=-1, keepdims=True)
    u = (x * lax.rsqrt(ms + RMS_EPS)) * g
    return u * (1.0 + scale) + shift


def _proj_kernel(h_ref, g_ref, sh_ref, sc_ref, cos_ref, sin_ref, w_ref, *out_refs, plan):
    ub = _norm_mod(h_ref[0], g_ref[...], sh_ref[0], sc_ref[0]).astype(BF16)
    cos = cos_ref[0]
    sin = sin_ref[0]
    lane = lax.broadcasted_iota(jnp.int32, cos.shape, 1)
    first = (lane & (HEAD_DIM - 1)) < HEAD_DIM // 2
    for (ws, width, rope, scale, oi, os_, act) in plan:
        for c0 in range(0, width, 256):
            cw = min(256, width - c0)
            y = jnp.dot(ub, w_ref[:, ws + c0: ws + c0 + cw], preferred_element_type=F32)
            for j in range(cw // LANE):
                blk = y[:, j * LANE:(j + 1) * LANE]
                if rope:
                    sw = jnp.where(first, pltpu.roll(blk, LANE - HEAD_DIM // 2, 1),
                                   pltpu.roll(blk, HEAD_DIM // 2, 1))
                    blk = blk * cos + sw * sin
                if scale != 1.0:
                    blk = blk * scale
                if act == "sigmoid":
                    blk = jax.nn.sigmoid(blk)
                elif act == "ones":
                    blk = jnp.where(lane == HEAD_DIM, 1.0, blk)
                elif act == "blockhot":
                    tok = pl.program_id(1) * cos.shape[0] + lax.broadcasted_iota(jnp.int32, cos.shape, 0)
                    blk = jnp.where(lane - HEAD_DIM == (tok >> SLC_SHIFT), 1.0, blk)
                if act == "spread":
                    o0 = os_ + 2 * (c0 + j * LANE)
                    low = lane < HEAD_DIM
                    out_refs[oi][0, :, o0:o0 + LANE] = jnp.where(low, blk, 0.0).astype(out_refs[oi].dtype)
                    out_refs[oi][0, :, o0 + LANE:o0 + 2 * LANE] = (
                        jnp.where(low, pltpu.roll(blk, HEAD_DIM, 1), 0.0).astype(out_refs[oi].dtype))
                    continue
                o0 = os_ + c0 + j * LANE
                out_refs[oi][0, :, o0:o0 + LANE] = blk.astype(out_refs[oi].dtype)


def _norm_proj(h, gain, shift, scale, cos_t, sin_t, w, plan, outs, tm=512):
    b, s, d = h.shape
    n = w.shape[1]
    vec = pl.BlockSpec((1, 1, d), lambda bi, i: (bi, 0, 0))
    return pl.pallas_call(
        functools.partial(_proj_kernel, plan=tuple(plan)),
        grid=(b, s // tm),
        in_specs=[pl.BlockSpec((1, tm, d), lambda bi, i: (bi, i, 0)),
                  pl.BlockSpec((1, d), lambda bi, i: (0, 0)),
                  vec, vec,
                  pl.BlockSpec((1, tm, LANE), lambda bi, i: (bi, i, 0)),
                  pl.BlockSpec((1, tm, LANE), lambda bi, i: (bi, i, 0)),
                  pl.BlockSpec((d, n), lambda bi, i: (0, 0))],
        out_specs=[pl.BlockSpec((1, tm, wd), lambda bi, i: (bi, i, 0)) for wd, _ in outs],
        out_shape=[jax.ShapeDtypeStruct((b, s, wd), dt) for wd, dt in outs],
        compiler_params=_cparams("arbitrary", "arbitrary"),
        name="norm_proj",
    )(h, gain.reshape(1, d), shift, scale, cos_t, sin_t, w)


def _ordered_float(u):
    key = u ^ INT_MIN
    return lax.bitcast_convert_type(key ^ ((key >> 31) & np.int32(0x7FFFFFFF)), F32)


def _indexer_kernel(iq_ref, ik_ref, iw_ref, tri_ref, bias_ref, sc_scr, iqs_scr, iwb_scr, *, n_keep):
    i = pl.program_id(1)
    q0 = i * TQI
    s_len = bias_ref.shape[2]
    n_all = s_len // TK
    nch = (q0 + TQI + TK - 1) // TK
    t = q0 + lax.broadcasted_iota(jnp.int32, (TQI, 1), 0)
    iw = iw_ref[0]
    for hh in range(IDX_HEADS):
        iqs_scr[hh * TQI:(hh + 1) * TQI, :] = iq_ref[0, :, hh * LANE:(hh + 1) * LANE]
        iwb_scr[hh] = jnp.broadcast_to(iw[:, hh:hh + 1], (TQI, LANE))

    def lanes_of(c, j):
        return pl.ds(pl.multiple_of(c * TK + j * LANE, LANE), LANE)

    def score_chunk(c, carry):
        r = lax.dot_general(iqs_scr[...], ik_ref[0, pl.ds(pl.multiple_of(c * TK, TK), TK), :], _NT,
                            preferred_element_type=F32)
        pos = c * TK + lax.broadcasted_iota(jnp.int32, (1, LANE), 1)
        for j in range(TK // LANE):
            acc = jnp.zeros((TQI, LANE), F32)
            for hh in range(IDX_HEADS):
                acc = acc + iwb_scr[hh] * jnp.maximum(r[hh * TQI:(hh + 1) * TQI, j * LANE:(j + 1) * LANE], 0.0)
            sc_scr[:, lanes_of(c, j)] = jnp.where(pos + j * LANE <= t, acc, MASKED)
        return carry

    lax.fori_loop(0, nch, score_chunk, 0)

    def lane_sum(x):
        return jnp.sum(x, axis=1, keepdims=True)

    assert n_all * (TK // LANE) < 256
    b8, b16 = np.int32(1 << 8), np.int32(1 << 16)
    blocks = list(range(0, TQI, RB))
    thr_u = [jnp.zeros((RB, 1), jnp.int32) for _ in blocks]
    cnt_thr = [jnp.zeros((RB, 1), F32) + (nch * TK).astype(F32) for _ in blocks]
    for bit in range(30, -1, -2):
        cus = [[thr_u[bi] | np.array([m << bit], np.uint32).view(np.int32)[0] for m in (1, 2, 3)]
               for bi in range(len(blocks))]
        cbs = [[jnp.broadcast_to(_ordered_float(u), (RB, LANE)) for u in cu] for cu in cus]
        parts = []
        for bi, rb in enumerate(blocks):
            def body(c, part, cb=cbs[bi], rb=rb):
                for j in range(TK // LANE):
                    x = sc_scr[rb:rb + RB, lanes_of(c, j)]
                    part = part + jnp.where(x >= cb[2], 1 + b8 + b16,
                                            jnp.where(x >= cb[1], 1 + b8, jnp.where(x >= cb[0], 1, 0)))
                return part

            parts.append(lax.fori_loop(0, nch, body, jnp.zeros((RB, LANE), jnp.int32)))
        for bi in range(len(blocks)):
            for m, sh in enumerate((0, 8, 16)):
                cnt = lane_sum(((parts[bi] >> sh) & 0xFF).astype(F32))
                ok = cnt >= n_keep
                thr_u[bi] = jnp.where(ok, cus[bi][m], thr_u[bi])
                cnt_thr[bi] = jnp.where(ok, cnt, cnt_thr[bi])

    thr, has_thr, tied = [], [], None
    for bi in range(len(blocks)):
        f = _ordered_float(thr_u[bi])
        has = f > MASKED
        thr.append(jnp.where(has, f, MASKED))
        has_thr.append(has)
        any_tie = jnp.max(jnp.where(has & (cnt_thr[bi] > n_keep), 1, 0)) > 0
        tied = any_tie if tied is None else jnp.logical_or(tied, any_tie)

    @pl.when(jnp.logical_not(tied))
    def _():
        for bi, rb in enumerate(blocks):
            thr_b = jnp.broadcast_to(thr[bi], (RB, LANE))

            def body(c, carry, thr_b=thr_b, rb=rb):
                for j in range(TK // LANE):
                    x = sc_scr[rb:rb + RB, lanes_of(c, j)]
                    bias_ref[0, rb:rb + RB, lanes_of(c, j)] = (
                        jnp.where((x >= thr_b) & (x > MASKED), 0.0, NEG_INF).astype(BF16))
                return carry
            lax.fori_loop(0, nch, body, 0)

    @pl.when(tied)
    def _():
        for bi, rb in enumerate(blocks):
            th, has = thr[bi], has_thr[bi]

            def count_gt(c, part, th=th, rb=rb):
                x = sc_scr[rb:rb + RB, pl.ds(pl.multiple_of(c * TK, TK), TK)]
                return part + lane_sum(jnp.where(x > th, 1.0, 0.0))
            need = n_keep - lax.fori_loop(0, nch, count_gt, jnp.zeros((RB, 1), F32))

            def body(c, seen, th=th, has=has, need=need, rb=rb):
                x = sc_scr[rb:rb + RB, pl.ds(pl.multiple_of(c * TK, TK), TK)]
                tie = x == th
                pref = jnp.dot(jnp.where(tie, 1.0, 0.0).astype(BF16), tri_ref[...],
                               preferred_element_type=F32) + seen
                sel = ((x > th) | (tie & (pref <= need) & has)) & (x > MASKED)
                bias_ref[0, rb:rb + RB, pl.ds(pl.multiple_of(c * TK, TK), TK)] = (
                    jnp.where(sel, 0.0, NEG_INF).astype(BF16))
                return pref[:, TK - 1:TK]
            lax.fori_loop(0, nch, body, jnp.zeros((RB, 1), F32))

    def fill(c, carry):
        bias_ref[0, :, pl.ds(pl.multiple_of(c * TK, TK), TK)] = jnp.full((TQI, TK), NEG_INF, BF16)
        return carry
    lax.fori_loop(nch, n_all, fill, 0)


def _dsa_bias(proj, iw, s_len):
    b = proj.shape[0]
    n_keep = min(DSA_TOPK, s_len // 4)
    tri = jnp.asarray(np.triu(np.ones((TK, TK), np.float32)), BF16)
    return pl.pallas_call(
        functools.partial(_indexer_kernel, n_keep=n_keep),
        grid=(b, s_len // TQI),
        in_specs=[pl.BlockSpec((1, TQI, IDX_HEADS * LANE), lambda bi, i: (bi, i, 2)),
                  pl.BlockSpec((1, s_len, LANE), lambda bi, i: (bi, 0, 32)),
                  pl.BlockSpec((1, TQI, LANE), lambda bi, i: (bi, i, 0)),
                  pl.BlockSpec((TK, TK), lambda bi, i: (0, 0))],
        out_specs=pl.BlockSpec((1, TQI, s_len), lambda bi, i: (bi, i, 0)),
        out_shape=jax.ShapeDtypeStruct((b, s_len, s_len), BF16),
        scratch_shapes=[pltpu.VMEM((TQI, s_len), F32),
                        pltpu.VMEM((IDX_HEADS * TQI, LANE), BF16),
                        pltpu.VMEM((IDX_HEADS, TQI, LANE), F32)],
        compiler_params=_cparams("arbitrary", "arbitrary"),
        name="dsa_indexer",
    )(proj, proj, iw, tri)


def _stack_heads(q_ref, qs_scr):
    for g in range(N_KV_HEADS):
        for r in range(GROUP):
            hh = g * GROUP + r
            qs_scr[g, r * TQ:(r + 1) * TQ, :] = q_ref[0, :, hh * LANE:(hh + 1) * LANE]


def _online_update(s, v, m_scr, acc_scr, g):
    m_old = m_scr[g]
    m_new = jnp.maximum(m_old, jnp.max(s, axis=1, keepdims=True))
    alpha = jnp.exp2(m_old - m_new)
    p = jnp.exp2(s - jnp.concatenate([m_new] * (s.shape[1] // LANE), axis=1))
    acc_scr[g] = alpha * acc_scr[g] + jnp.dot(p.astype(BF16), v, preferred_element_type=F32)
    m_scr[g] = m_new


def _init_online(m_scr, acc_scr):
    m_scr[...] = jnp.full(m_scr.shape, M_INIT, F32)
    acc_scr[...] = jnp.zeros(acc_scr.shape, F32)


def _normalised(acc):
    return acc / acc[:, HEAD_DIM:HEAD_DIM + 1]


def _head(g):
    return slice(g * LANE, (g + 1) * LANE)


def _store_head_pairs(o_ref, g, heads):
    low = lax.broadcasted_iota(jnp.int32, heads[0].shape, 1) < HEAD_DIM
    for j in range(GROUP // 2):
        pair = g * (GROUP // 2) + j
        both = jnp.where(low, heads[2 * j], pltpu.roll(heads[2 * j + 1], HEAD_DIM, 1))
        o_ref[0, :, pair * LANE:(pair + 1) * LANE] = both.astype(BF16)


def _dsa_attn_kernel(q_ref, k_ref, v_ref, bias_ref, o_ref, qs_scr, m_scr, acc_scr):
    i = pl.program_id(1)
    nch = (i * TQ + TQ + TK - 1) // TK
    _stack_heads(q_ref, qs_scr)
    _init_online(m_scr, acc_scr)

    def chunk(c, carry):
        k0 = pl.multiple_of(c * TK, TK)
        bias = bias_ref[0, :, pl.ds(k0, TK)].astype(F32)
        bias4 = jnp.concatenate([bias] * GROUP, axis=0)
        for g in range(N_KV_HEADS):
            s = lax.dot_general(qs_scr[g], k_ref[0, pl.ds(k0, TK), _head(g)], _NT,
                                preferred_element_type=F32) + bias4
            _online_update(s, v_ref[0, pl.ds(k0, TK), _head(g)], m_scr, acc_scr, g)
        return carry

    lax.fori_loop(0, nch, chunk, 0)
    for g in range(N_KV_HEADS):
        og = _normalised(acc_scr[g])
        _store_head_pairs(o_ref, g, [og[r * TQ:(r + 1) * TQ] for r in range(GROUP)])


def _dsa_attention(proj, bias):
    b, s_len, _ = proj.shape
    return pl.pallas_call(
        _dsa_attn_kernel,
        grid=(b, s_len // TQ),
        in_specs=[pl.BlockSpec((1, TQ, N_HEADS * LANE), lambda bi, i: (bi, i, 0)),
                  pl.BlockSpec((1, s_len, N_KV_HEADS * LANE), lambda bi, i: (bi, 0, 6)),
                  pl.BlockSpec((1, s_len, N_KV_HEADS * LANE), lambda bi, i: (bi, 0, 7)),
                  pl.BlockSpec((1, TQ, s_len), lambda bi, i: (bi, i, 0))],
        out_specs=pl.BlockSpec((1, TQ, N_HEADS * HEAD_DIM), lambda bi, i: (bi, i, 0)),
        out_shape=jax.ShapeDtypeStruct((b, s_len, N_HEADS * HEAD_DIM), BF16),
        scratch_shapes=[pltpu.VMEM((N_KV_HEADS, GROUP * TQ, LANE), BF16),
                        pltpu.VMEM((N_KV_HEADS, GROUP * TQ, LANE), F32),
                        pltpu.VMEM((N_KV_HEADS, GROUP * TQ, LANE), F32)],
        compiler_params=_cparams("arbitrary", "arbitrary"),
        name="dsa_attention",
    )(proj, proj, proj, bias)


def _outproj_kernel(o_ref, w_ref, gate_ref, h_ref, out_ref):
    y = jnp.dot(o_ref[0], w_ref[...], preferred_element_type=F32)
    out_ref[0] = h_ref[0] + gate_ref[0] * y


def _outproj_residual(o, w, gate, h, tm=512):
    b, s, d = h.shape
    k = o.shape[2]
    return pl.pallas_call(
        _outproj_kernel,
        grid=(b, s // tm),
        in_specs=[pl.BlockSpec((1, tm, k), lambda bi, i: (bi, i, 0)),
                  pl.BlockSpec((k, d), lambda bi, i: (0, 0)),
                  pl.BlockSpec((1, 1, d), lambda bi, i: (bi, 0, 0)),
                  pl.BlockSpec((1, tm, d), lambda bi, i: (bi, i, 0))],
        out_specs=pl.BlockSpec((1, tm, d), lambda bi, i: (bi, i, 0)),
        out_shape=jax.ShapeDtypeStruct((b, s, d), F32),
        compiler_params=_cparams("arbitrary", "arbitrary"),
        name="outproj_residual",
    )(o, w, gate, h)


def _ffn_kernel(h_ref, g_ref, sh_ref, sc_ref, gate_ref, wg_ref, wu_ref, wd_ref, out_ref, u_scr, acc_scr):
    f = pl.program_id(2)

    @pl.when(f == 0)
    def _():
        u_scr[...] = _norm_mod(h_ref[0], g_ref[...], sh_ref[0], sc_ref[0]).astype(BF16)
        acc_scr[...] = jnp.zeros(acc_scr.shape, F32)

    ub = u_scr[...]
    a = jnp.dot(ub, wg_ref[...], preferred_element_type=F32)
    up = jnp.dot(ub, wu_ref[...], preferred_element_type=F32)
    hid = (a * jax.nn.sigmoid(a)) * up
    acc_scr[...] += jnp.dot(hid.astype(BF16), wd_ref[...], preferred_element_type=F32)

    @pl.when(f == pl.num_programs(2) - 1)
    def _():
        out_ref[0] = h_ref[0] + gate_ref[0] * acc_scr[...]


def _ffn_residual(h, gain, shift, scale, gate, wg, wu, wd, tm=512):
    b, s, d = h.shape
    dff = wg.shape[1]
    tf = dff
    vec = pl.BlockSpec((1, 1, d), lambda bi, i, f: (bi, 0, 0))
    once = pl.Buffered(1)
    return pl.pallas_call(
        _ffn_kernel,
        grid=(b, s // tm, dff // tf),
        in_specs=[pl.BlockSpec((1, tm, d), lambda bi, i, f: (bi, i, 0)),
                  pl.BlockSpec((1, d), lambda bi, i, f: (0, 0)),
                  vec, vec, vec,
                  pl.BlockSpec((d, tf), lambda bi, i, f: (0, f), pipeline_mode=once),
                  pl.BlockSpec((d, tf), lambda bi, i, f: (0, f), pipeline_mode=once),
                  pl.BlockSpec((tf, d), lambda bi, i, f: (f, 0), pipeline_mode=once)],
        out_specs=pl.BlockSpec((1, tm, d), lambda bi, i, f: (bi, i, 0)),
        out_shape=jax.ShapeDtypeStruct((b, s, d), F32),
        scratch_shapes=[pltpu.VMEM((tm, d), BF16), pltpu.VMEM((tm, d), F32)],
        compiler_params=_cparams("arbitrary", "arbitrary", "arbitrary"),
        name="ffn_residual",
    )(h, gain.reshape(1, d), shift, scale, gate, wg, wu, wd)


def _compress_kernel(x_ref, pe_ref, w1_ref, w2_ref, o_ref):
    x = x_ref[0, 0, 0]
    nr = x.shape[0]
    pa = jnp.dot((x + pe_ref[0, 0]).astype(BF16), w1_ref[0, 0], preferred_element_type=F32)
    pb = jnp.dot((x + pe_ref[0, 1]).astype(BF16), w1_ref[0, 1], preferred_element_type=F32)
    hid = pa + pltpu.roll(pb, nr - 1, 0)
    act = 0.5 * hid * (1.0 + jnp.tanh(np.float32(np.sqrt(2.0 / np.pi)) * (hid + 0.044715 * (hid * hid * hid))))
    o_ref[0, 0, 0] = jnp.dot(act.astype(BF16), w2_ref[0], preferred_element_type=F32).astype(BF16)


def _compress(x, pe, w1, w2):
    two, b, g, nr, kk = x.shape
    return pl.pallas_call(
        _compress_kernel,
        grid=(two, b, g),
        in_specs=[pl.BlockSpec((1, 1, 1, nr, kk), lambda a, bi, gi: (a, bi, gi, 0, 0)),
                  pl.BlockSpec((1, 2, 1, kk), lambda a, bi, gi: (a, 0, 0, 0)),
                  pl.BlockSpec((1, 2, kk, CMP_HIDDEN), lambda a, bi, gi: (a, 0, 0, 0)),
                  pl.BlockSpec((1, CMP_HIDDEN, HEAD_DIM), lambda a, bi, gi: (a, 0, 0))],
        out_specs=pl.BlockSpec((1, 1, 1, nr, HEAD_DIM), lambda a, bi, gi: (a, bi, gi, 0, 0)),
        out_shape=jax.ShapeDtypeStruct((two, b, g, nr, HEAD_DIM), BF16),
        compiler_params=_cparams("arbitrary", "arbitrary", "arbitrary"),
        name="nsa_compress",
    )(x, pe, w1, w2)


def _nsa_kernel(q_ref, gt_ref, kc_ref, vc_ref, ks_ref, kw_ref, vs_ref, vw_ref, aggt_ref, o_ref,
                qs_scr, oc_scr, m_scr, acc_scr, *, n_sel):
    i = pl.program_id(1)
    q0 = i * TQ
    nch = (q0 + TQ + TK - 1) // TK
    _stack_heads(q_ref, qs_scr)
    _init_online(m_scr, acc_scr)
    t = q0 + lax.broadcasted_iota(jnp.int32, (TQ, 1), 0)
    t4 = jnp.concatenate([t] * GROUP, axis=0)

    n_cmp = kc_ref.shape[1]
    cmp_end = lax.broadcasted_iota(jnp.int32, (1, n_cmp), 1) * CMP_STRIDE + (CMP_LEN - 1)
    cmask = cmp_end <= t4
    n_blk = aggt_ref.shape[0]
    jrow = lax.broadcasted_iota(jnp.int32, (n_blk, TQ), 0)
    tq = q0 + lax.broadcasted_iota(jnp.int32, (n_blk, TQ), 1)
    jt = tq >> SLC_SHIFT
    valid = jrow * SLC_LEN <= tq
    forced = (jrow == 0) | (jrow == jt) | (jrow == jt - 1)
    for g in range(N_KV_HEADS):
        sc = lax.dot_general(qs_scr[g], kc_ref[0, :, g * LANE:(g + 1) * LANE], _NT, preferred_element_type=F32)
        sm = jnp.where(cmask, sc, NEG_INF)
        e = jnp.where(cmask, jnp.exp2(sm - jnp.max(sm, axis=1, keepdims=True)), 0.0)
        pc = e / jnp.maximum(jnp.sum(e, axis=1, keepdims=True), 1e-30)
        oc_scr[g] = jnp.dot(pc.astype(BF16), vc_ref[0, :, _head(g)], preferred_element_type=F32)
        pcs = pc[0:TQ]
        for r in range(1, GROUP):
            pcs = pcs + pc[r * TQ:(r + 1) * TQ]
        imp = lax.dot_general(aggt_ref[...], pcs, _NT, preferred_element_type=F32,
                              precision=lax.Precision.HIGHEST)
        imp = jnp.where(valid, jnp.where(forced, POS_INF, imp), NEG_INF)
        rank = jnp.zeros((n_blk, TQ), jnp.int32)
        for j in range(n_blk):
            row = imp[j:j + 1, :]
            beats = (row > imp) | ((row == imp) & (jrow > j))
            rank = rank + beats.astype(jnp.int32)
        sel = (rank < n_sel) & valid
        selb = jnp.where(sel, 0.0, NEG_INF)
        pads = [jnp.zeros((HEAD_DIM, TQ), F32), selb]
        if n_blk < LANE - HEAD_DIM:
            pads.append(jnp.zeros((LANE - HEAD_DIM - n_blk, TQ), F32))
        selb = jnp.concatenate(pads, axis=0)
        selb = selb.T.astype(BF16)
        qs_scr[g] = qs_scr[g] + jnp.concatenate([selb] * GROUP, axis=0)

    def slc_chunk(c, diagonal):
        k0 = pl.multiple_of(c * TK, TK)
        for g in range(N_KV_HEADS):
            s = lax.dot_general(qs_scr[g], ks_ref[0, pl.ds(k0, TK), _head(g)], _NT,
                                preferred_element_type=F32)
            if diagonal:
                pos = k0 + lax.broadcasted_iota(jnp.int32, (1, TK), 1)
                s = jnp.where(pos <= t4, s, NEG_INF)
            _online_update(s, vs_ref[0, pl.ds(k0, TK), _head(g)], m_scr, acc_scr, g)

    def slc_body(c, carry):
        slc_chunk(c, False)
        return carry

    lax.fori_loop(0, nch - 1, slc_body, 0)
    slc_chunk(nch - 1, True)

    gt = gt_ref[0]
    w0 = pl.multiple_of(jnp.maximum(q0 - WINDOW, 0), TQ)
    wpos = w0 + lax.broadcasted_iota(jnp.int32, (1, WIN_SPAN), 1)
    wmask = (wpos <= t4) & (wpos > t4 - WINDOW)
    for g in range(N_KV_HEADS):
        s = lax.dot_general(qs_scr[g], kw_ref[0, pl.ds(w0, WIN_SPAN), _head(g)], _NT,
                            preferred_element_type=F32)
        s = jnp.where(wmask, s, NEG_INF)
        p = jnp.exp2(s - jnp.max(s, axis=1, keepdims=True))
        p = jnp.where(wmask, p, 0.0)
        ow = _normalised(jnp.dot(p.astype(BF16), vw_ref[0, pl.ds(w0, WIN_SPAN), _head(g)],
                                 preferred_element_type=F32))
        osl = _normalised(acc_scr[g])
        oc = oc_scr[g]
        heads = []
        for r in range(GROUP):
            hh = g * GROUP + r
            rs = slice(r * TQ, (r + 1) * TQ)
            heads.append(gt[:, hh:hh + 1] * oc[rs] + gt[:, N_HEADS + hh:N_HEADS + hh + 1] * osl[rs]
                         + gt[:, 2 * N_HEADS + hh:2 * N_HEADS + hh + 1] * ow[rs])
        _store_head_pairs(o_ref, g, heads)


def _nsa_attention(q, gates, kc, vc, kv, agg_t):
    b, s_len, _ = q.shape
    n_cmp = kc.shape[1]
    n_blk = agg_t.shape[0]
    n_sel = min(SLC_TOPN, s_len // SLC_LEN)
    rows = GROUP * TQ
    return pl.pallas_call(
        functools.partial(_nsa_kernel, n_sel=n_sel),
        grid=(b, s_len // TQ),
        in_specs=[pl.BlockSpec((1, TQ, N_HEADS * LANE), lambda bi, i: (bi, i, 0)),
                  pl.BlockSpec((1, TQ, LANE), lambda bi, i: (bi, i, 0)),
                  pl.BlockSpec((1, n_cmp, N_KV_HEADS * LANE), lambda bi, i: (bi, 0, 0)),
                  pl.BlockSpec((1, n_cmp, N_KV_HEADS * LANE), lambda bi, i: (bi, 0, 0)),
                  pl.BlockSpec((1, s_len, N_KV_HEADS * LANE), lambda bi, i: (bi, 0, 0), pipeline_mode=pl.Buffered(1)),
                  pl.BlockSpec((1, s_len, N_KV_HEADS * LANE), lambda bi, i: (bi, 0, 1), pipeline_mode=pl.Buffered(1)),
                  pl.BlockSpec((1, s_len, N_KV_HEADS * LANE), lambda bi, i: (bi, 0, 2), pipeline_mode=pl.Buffered(1)),
                  pl.BlockSpec((1, s_len, N_KV_HEADS * LANE), lambda bi, i: (bi, 0, 3), pipeline_mode=pl.Buffered(1)),
                  pl.BlockSpec((n_blk, n_cmp), lambda bi, i: (0, 0))],
        out_specs=pl.BlockSpec((1, TQ, N_HEADS * HEAD_DIM), lambda bi, i: (bi, i, 0)),
        out_shape=jax.ShapeDtypeStruct((b, s_len, N_HEADS * HEAD_DIM), BF16),
        scratch_shapes=[pltpu.VMEM((N_KV_HEADS, rows, LANE), BF16),
                        pltpu.VMEM((N_KV_HEADS, rows, LANE), F32),
                        pltpu.VMEM((N_KV_HEADS, rows, LANE), F32),
                        pltpu.VMEM((N_KV_HEADS, rows, LANE), F32)],
        compiler_params=_cparams("arbitrary", "arbitrary"),
        name="nsa_attention",
    )(q, gates, kc, vc, kv, kv, kv, kv, agg_t)


def _router_kernel(h_ref, g_ref, sh_ref, sc_ref, wr_ref, low_ref, u_ref, info_ref, cnt_ref, carry_scr):
    @pl.when((pl.program_id(0) == 0) & (pl.program_id(1) == 0))
    def _():
        carry_scr[...] = jnp.zeros(carry_scr.shape, F32)

    u = _norm_mod(h_ref[0], g_ref[...], sh_ref[0], sc_ref[0])
    u_ref[0] = u.astype(BF16)
    logits = jnp.dot(u, wr_ref[...], preferred_element_type=F32, precision=lax.Precision.HIGHEST)
    lane = lax.broadcasted_iota(jnp.int32, logits.shape, 1)
    logits = jnp.where(lane < N_EXPERTS, logits, -jnp.inf)
    m1 = jnp.max(logits, axis=1, keepdims=True)
    i1 = jnp.min(jnp.where(logits == m1, lane, LANE), axis=1, keepdims=True)
    rest = jnp.where(lane == i1, -jnp.inf, logits)
    m2 = jnp.max(rest, axis=1, keepdims=True)
    i2 = jnp.min(jnp.where(rest == m2, lane, LANE), axis=1, keepdims=True)
    e2 = jnp.exp(m2 - m1)
    w1 = 1.0 / (1.0 + e2)
    w2 = e2 / (1.0 + e2)
    hit1 = lane == i1
    hit2 = lane == i2
    onehot = jnp.where(hit1 | hit2, 1.0, 0.0).astype(BF16)
    ahead = jnp.dot(low_ref[...], onehot, preferred_element_type=F32) + carry_scr[0:1, :]
    pos1 = jnp.sum(jnp.where(hit1, ahead, 0.0), axis=1, keepdims=True)
    pos2 = jnp.sum(jnp.where(hit2, ahead, 0.0), axis=1, keepdims=True)
    tile_cnt = jnp.dot(jnp.ones((8, TMX), BF16), onehot, preferred_element_type=F32)
    cnt_ref[0] = tile_cnt
    carry_scr[...] += tile_cnt
    info = jnp.zeros(logits.shape, F32)
    for j, v in enumerate((i1.astype(F32), i2.astype(F32), pos1, pos2, w1, w2)):
        info = jnp.where(lane == j, v, info)
    info_ref[0] = info


def _moe_gather_kernel(gb_ref, gs_ref, u_ref, dr_ref, w_ref, xs_ref, ws_ref):
    d = pl.program_id(0)
    xs_ref[...] = jnp.zeros(xs_ref.shape, BF16)
    ws_ref[...] = jnp.zeros(ws_ref.shape, F32)
    row = d * RBLK + lax.broadcasted_iota(jnp.int32, (RBLK, TMX), 0)

    def pair(k, carry):
        s = gs_ref[k]
        p1 = dr_ref[s, 0:1, :] == row
        p2 = dr_ref[s, 1:2, :] == row
        onehot = jnp.where(p1 | p2, 1.0, 0.0).astype(BF16)
        tok = u_ref[pl.ds(pl.multiple_of(s * TMX, TMX), TMX), :]
        xs_ref[...] += jnp.dot(onehot, tok, preferred_element_type=F32).astype(BF16)
        wrow = jnp.where(p1, w_ref[s, 0:1, :], 0.0) + jnp.where(p2, w_ref[s, 1:2, :], 0.0)
        ws_ref[...] += jnp.sum(wrow, axis=1, keepdims=True)
        return carry

    lax.fori_loop(gb_ref[d], gb_ref[d + 1], pair, 0)


def _moe_ffn_kernel(te_ref, nt_ref, x_ref, ws_ref, wg_ref, wu_ref, wd_ref, o_ref, acc_scr):
    i = pl.program_id(0)
    f = pl.program_id(1)
    live = i < nt_ref[0]

    @pl.when(f == 0)
    def _():
        acc_scr[...] = jnp.zeros(acc_scr.shape, F32)

    @pl.when(live)
    def _():
        x = x_ref[...]
        a = jnp.dot(x, wg_ref[0], preferred_element_type=F32)
        up = jnp.dot(x, wu_ref[0], preferred_element_type=F32)
        hid = (a * jax.nn.sigmoid(a)) * up
        acc_scr[...] += jnp.dot(hid.astype(BF16), wd_ref[0], preferred_element_type=F32)

    @pl.when(f == pl.num_programs(1) - 1)
    def _():
        w = jnp.concatenate([ws_ref[...]] * (acc_scr.shape[1] // LANE), axis=1)
        o_ref[...] = jnp.where(live, acc_scr[...] * w, 0.0).astype(BF16)


def _moe_combine_kernel(ys_ref, yd_ref, nv_ref, o_ref, dr_ref, h_ref, gate_ref, fg_ref, out_ref, y_scr):
    k = pl.program_id(0)
    last_k = pl.num_programs(0) - 1
    s = ys_ref[k]
    d = yd_ref[k]

    @pl.when((k == 0) | (ys_ref[jnp.maximum(k - 1, 0)] != s))
    def _():
        y_scr[...] = jnp.zeros(y_scr.shape, F32)

    @pl.when(k < nv_ref[0])
    def _():
        col = d * RBLK + lax.broadcasted_iota(jnp.int32, (TMX, RBLK), 1)
        dr = dr_ref[...]
        onehot = jnp.where((dr[:, 0:1] == col) | (dr[:, 1:2] == col), 1.0, 0.0).astype(BF16)
        y_scr[...] += jnp.dot(onehot, o_ref[...], preferred_element_type=F32)

    @pl.when((k == last_k) | (ys_ref[jnp.minimum(k + 1, last_k)] != s))
    def _():
        hn = h_ref[0] + gate_ref[0] * y_scr[...]
        ms = jnp.mean(hn * hn, axis=-1, keepdims=True)
        out_ref[0] = (hn * lax.rsqrt(ms + RMS_EPS)) * fg_ref[...]


def _moe_plan(info, cnt, n_tok):
    nt = n_tok // TMX
    n_blk = (TOP_K_EXPERTS * n_tok + N_EXPERTS * GRP) // RBLK
    n_pairs = n_blk + N_EXPERTS * nt
    e1, e2, pos1, pos2 = [info[:, j].astype(jnp.int32) for j in range(4)]
    counts = cnt[:, 0, :N_EXPERTS].astype(jnp.int32)
    padded = (jnp.sum(counts, axis=0) + GRP - 1) // GRP * GRP
    ends = jnp.cumsum(padded)
    start = ends - padded
    dr1 = start[e1] + pos1
    dr2 = start[e2] + pos2
    first = start[None, :] + jnp.cumsum(counts, axis=0) - counts
    last = first + counts - 1
    fb = first // RBLK
    nb = jnp.where(counts > 0, last // RBLK - fb + 1, 0)
    k3 = jnp.arange(3)
    cd = (fb[..., None] + k3).reshape(-1)
    cs = jnp.broadcast_to(jnp.arange(nt)[:, None, None], (nt, N_EXPERTS, 3)).reshape(-1)
    cv = (k3 < nb[..., None]).reshape(-1)
    n_valid = jnp.sum(cv.astype(jnp.int32))

    def ordered(key):
        smaller = cv[None, :] & (key[None, :] < key[:, None])
        slot = jnp.where(cv, jnp.sum(smaller.astype(jnp.int32), axis=1), -1)
        want = jnp.minimum(jnp.arange(n_pairs), n_valid - 1)
        pick = slot[None, :] == want[:, None]
        return (jnp.sum(jnp.where(pick, cd[None, :], 0), axis=1).astype(jnp.int32),
                jnp.sum(jnp.where(pick, cs[None, :], 0), axis=1).astype(jnp.int32))

    gd, gs = ordered(cd * nt + cs)
    blocks = jnp.arange(n_blk + 1)
    gb = jnp.sum((cv[None, :] & (cd[None, :] < blocks[:, None])).astype(jnp.int32), axis=1)
    yd, ys = ordered(cs * n_blk + cd)
    n_ffn = n_blk * RBLK // GRP
    tile_row = jnp.arange(n_ffn) * GRP
    tile_e = jnp.minimum(jnp.sum((ends[None, :] <= tile_row[:, None]).astype(jnp.int32), axis=1), N_EXPERTS - 1)
    return dict(dr1=dr1, dr2=dr2, gb=gb.astype(jnp.int32), gs=gs, yd=yd, ys=ys, n_valid=n_valid.reshape(1),
                tile_e=tile_e.astype(jnp.int32), n_live=(ends[-1] // GRP).astype(jnp.int32).reshape(1),
                n_blk=n_blk, n_pairs=n_pairs, n_ffn=n_ffn)


def _moe_residual_norm(h, gain, shift, scale, gate, w_router, final_gain, wg, wu, wd, tf=1792):
    b, s, d = h.shape
    n_tok = b * s
    tpb = s // TMX
    nt = n_tok // TMX
    ne, _, dff = wg.shape
    vec = pl.BlockSpec((1, 1, d), lambda bi, i: (bi, 0, 0))
    low = jnp.asarray(np.tril(np.ones((TMX, TMX), np.float32), -1), BF16)
    u, info, cnt = pl.pallas_call(
        _router_kernel,
        grid=(b, tpb),
        in_specs=[pl.BlockSpec((1, TMX, d), lambda bi, i: (bi, i, 0)),
                  pl.BlockSpec((1, d), lambda bi, i: (0, 0)),
                  vec, vec,
                  pl.BlockSpec((d, LANE), lambda bi, i: (0, 0)),
                  pl.BlockSpec((TMX, TMX), lambda bi, i: (0, 0))],
        out_specs=[pl.BlockSpec((1, TMX, d), lambda bi, i: (bi, i, 0)),
                   pl.BlockSpec((1, TMX, LANE), lambda bi, i: (bi, i, 0)),
                   pl.BlockSpec((1, 8, LANE), lambda bi, i: (bi * tpb + i, 0, 0))],
        out_shape=[jax.ShapeDtypeStruct((b, s, d), BF16),
                   jax.ShapeDtypeStruct((b, s, LANE), F32),
                   jax.ShapeDtypeStruct((nt, 8, LANE), F32)],
        scratch_shapes=[pltpu.VMEM((8, LANE), F32)],
        compiler_params=_cparams("arbitrary", "arbitrary"),
        name="moe_router",
    )(h, gain.reshape(1, d), shift, scale, w_router, low)

    info = info.reshape(n_tok, LANE)
    plan = _moe_plan(info, cnt, n_tok)
    n_blk, n_pairs, n_ffn = plan["n_blk"], plan["n_pairs"], plan["n_ffn"]
    pad6 = jnp.zeros((nt, 6, TMX), jnp.int32)
    dr_rows = jnp.concatenate([plan["dr1"].reshape(nt, 1, TMX), plan["dr2"].reshape(nt, 1, TMX), pad6], axis=1)
    w_rows = jnp.concatenate([info[:, 4].reshape(nt, 1, TMX), info[:, 5].reshape(nt, 1, TMX),
                              pad6.astype(F32)], axis=1)
    dr_cols = jnp.pad(jnp.stack([plan["dr1"], plan["dr2"]], axis=1), ((0, 0), (0, LANE - 2)))

    xs, ws = pl.pallas_call(
        _moe_gather_kernel,
        grid_spec=pltpu.PrefetchScalarGridSpec(
            num_scalar_prefetch=2,
            grid=(n_blk,),
            in_specs=[pl.BlockSpec((n_tok, d), lambda k, gb, gs: (0, 0), pipeline_mode=pl.Buffered(1)),
                      pl.BlockSpec((nt, 8, TMX), lambda k, gb, gs: (0, 0, 0), pipeline_mode=pl.Buffered(1)),
                      pl.BlockSpec((nt, 8, TMX), lambda k, gb, gs: (0, 0, 0), pipeline_mode=pl.Buffered(1))],
            out_specs=[pl.BlockSpec((RBLK, d), lambda k, gb, gs: (k, 0)),
                       pl.BlockSpec((RBLK, LANE), lambda k, gb, gs: (k, 0))]),
        out_shape=[jax.ShapeDtypeStruct((n_blk * RBLK, d), BF16),
                   jax.ShapeDtypeStruct((n_blk * RBLK, LANE), F32)],
        compiler_params=_cparams("arbitrary"),
        name="moe_gather",
    )(plan["gb"], plan["gs"], u.reshape(n_tok, d), dr_rows, w_rows)

    def live_tile(i, te, nl):
        return jnp.maximum(jnp.minimum(i, nl[0] - 1), 0)

    rows_out = pl.pallas_call(
        _moe_ffn_kernel,
        grid_spec=pltpu.PrefetchScalarGridSpec(
            num_scalar_prefetch=2,
            grid=(n_ffn, dff // tf),
            in_specs=[pl.BlockSpec((GRP, d), lambda i, f, te, nl: (live_tile(i, te, nl), 0)),
                      pl.BlockSpec((GRP, LANE), lambda i, f, te, nl: (live_tile(i, te, nl), 0)),
                      pl.BlockSpec((1, d, tf), lambda i, f, te, nl: (te[i], 0, f)),
                      pl.BlockSpec((1, d, tf), lambda i, f, te, nl: (te[i], 0, f)),
                      pl.BlockSpec((1, tf, d), lambda i, f, te, nl: (te[i], f, 0))],
            out_specs=pl.BlockSpec((GRP, d), lambda i, f, te, nl: (i, 0)),
            scratch_shapes=[pltpu.VMEM((GRP, d), F32)]),
        out_shape=jax.ShapeDtypeStruct((n_blk * RBLK, d), BF16),
        compiler_params=_cparams("arbitrary", "arbitrary"),
        name="moe_experts",
    )(plan["tile_e"], plan["n_live"], xs, ws, wg, wu, wd)

    return pl.pallas_call(
        _moe_combine_kernel,
        grid_spec=pltpu.PrefetchScalarGridSpec(
            num_scalar_prefetch=3,
            grid=(n_pairs,),
            in_specs=[pl.BlockSpec((RBLK, d), lambda k, ys, yd, nv: (yd[k], 0)),
                      pl.BlockSpec((TMX, LANE), lambda k, ys, yd, nv: (ys[k], 0)),
                      pl.BlockSpec((1, TMX, d), lambda k, ys, yd, nv: (ys[k] // tpb, ys[k] % tpb, 0)),
                      pl.BlockSpec((1, 1, d), lambda k, ys, yd, nv: (ys[k] // tpb, 0, 0)),
                      pl.BlockSpec((1, d), lambda k, ys, yd, nv: (0, 0))],
            out_specs=pl.BlockSpec((1, TMX, d), lambda k, ys, yd, nv: (ys[k] // tpb, ys[k] % tpb, 0)),
            scratch_shapes=[pltpu.VMEM((TMX, d), F32)]),
        out_shape=jax.ShapeDtypeStruct((b, s, d), F32),
        compiler_params=_cparams("arbitrary"),
        name="moe_combine_norm",
    )(plan["ys"], plan["yd"], plan["n_valid"], rows_out, dr_cols, h, gate, final_gain.reshape(1, d))


def _pad_heads(w, n_heads):
    d = w.shape[0]
    w = w.reshape(d, n_heads, HEAD_DIM)
    return jnp.pad(w, ((0, 0), (0, 0), (0, LANE - HEAD_DIM))).reshape(d, n_heads * LANE)


def _pad_cols(w, n):
    return jnp.pad(w, ((0, 0), (0, n - w.shape[1])))


def _dsa_weights(w_in):
    q, k, v, iq, ik, iw = jnp.split(w_in, [1024, 1280, 1536, 2048, 2112], axis=1)
    w = jnp.concatenate([q, iq, _pad_heads(k, N_KV_HEADS), _pad_heads(v, N_KV_HEADS),
                         _pad_cols(ik, LANE), _pad_cols(iw, LANE)], axis=1).astype(BF16)
    plan = [(0, 1024, True, Q_SCALE, 0, 0, "spread"),
            (1024, 512, True, 1.0, 0, 2048, "spread"),
            (1536, 512, True, 1.0, 0, 3072, None),
            (2048, 512, False, 1.0, 0, 3584, "ones"),
            (2560, 128, True, 1.0, 0, 4096, None),
            (2688, 128, False, IDX_HEADS ** -0.5 * IDX_DIM ** -0.5, 1, 0, None)]
    return w, plan, [(4224, BF16), (LANE, F32)]


def _nsa_q_weights(w_q):
    q = w_q[:, :N_HEADS * HEAD_DIM]
    gates = w_q[:, N_HEADS * HEAD_DIM:].reshape(-1, N_HEADS, N_NSA_BRANCH)
    gates = jnp.transpose(gates, (0, 2, 1)).reshape(-1, N_NSA_BRANCH * N_HEADS)
    w = jnp.concatenate([q, _pad_cols(gates, LANE)], axis=1).astype(BF16)
    plan = [(0, 1024, True, Q_SCALE, 0, 0, "spread"),
            (1024, 128, False, 1.0, 1, 0, "sigmoid")]
    return w, plan, [(2048, BF16), (LANE, F32)]


def _kv_weights(w_kv):
    kvw = w_kv.reshape(-1, 2 * N_NSA_BRANCH, N_KV_HEADS * HEAD_DIM)
    k_cmp, v_cmp, k_slc, v_slc, k_win, v_win = [kvw[:, j] for j in range(2 * N_NSA_BRANCH)]
    w = jnp.concatenate([_pad_heads(k_slc, N_KV_HEADS), _pad_heads(k_win, N_KV_HEADS),
                         _pad_heads(v_slc, N_KV_HEADS), _pad_heads(v_win, N_KV_HEADS),
                         k_cmp, v_cmp], axis=1).astype(BF16)
    plan = [(0, 512, True, 1.0, 0, 0, "blockhot"),
            (512, 512, True, 1.0, 0, 512, None),
            (1024, 512, False, 1.0, 0, 1024, "ones"),
            (1536, 512, False, 1.0, 0, 1536, "ones"),
            (2048, 256, True, 1.0, 1, 0, None),
            (2304, 256, False, 1.0, 1, 256, None)]
    return w, plan, [(2048, BF16), (512, F32)]


def _nsa_constants(s_len):
    n_cmp = (s_len - CMP_LEN) // CMP_STRIDE + 1
    n_slc = s_len // SLC_LEN
    cmp_start = np.arange(n_cmp) * CMP_STRIDE
    slc_start = np.arange(n_slc) * SLC_LEN
    ov = (np.minimum(cmp_start[:, None] + CMP_LEN, slc_start[None, :] + SLC_LEN)
          - np.maximum(cmp_start[:, None], slc_start[None, :]))
    agg = (np.clip(ov, 0, None) / CMP_LEN).astype(np.float32)
    agg_t = np.zeros((n_slc, s_len // CMP_STRIDE), np.float32)
    agg_t[:, :n_cmp] = agg.T
    return jnp.asarray(agg_t)


def kernel(x, c, positions, attn_gain, ffn_gain, w_ada, b_ada, a_w_in, a_w_out, b_w_q, b_w_out, kv_gain, w_kv_ada, b_kv_ada, w_kv, cmp_pe_k, cmp_w1_k, cmp_w2_k, cmp_pe_v, cmp_w1_v, cmp_w2_v, ffn_w_gate, ffn_w_up, ffn_w_down, moe_w_router, moe_w_gate, moe_w_up, moe_w_down, final_gain):
    b, s_len, d = x.shape
    assert s_len % TK == 0 and s_len // SLC_LEN <= HEAD_DIM and s_len >= WIN_SPAN and b <= 8

    cos_t, sin_t = _rope_tables(positions)
    c_pad = jnp.pad(c, ((0, 8 - b), (0, 0)))
    mod = _ada(c_pad, w_ada, b_ada)[:, :b]
    kv_mod = _ada(c_pad, w_kv_ada[None], b_kv_ada[None])[0, :b]

    def parts(m, n):
        return [p[:, None, :] for p in jnp.split(m, n, axis=-1)]

    a_shift, a_scale, a_gate, f_shift, f_scale, f_gate = parts(mod[0], 6)
    w, plan, outs = _dsa_weights(a_w_in[0])
    proj, iw = _norm_proj(x, attn_gain[0], a_shift, a_scale, cos_t, sin_t, w, plan, outs)
    bias = _dsa_bias(proj, iw, s_len)
    o = _dsa_attention(proj, bias)
    h = _outproj_residual(o, a_w_out[0].astype(BF16), a_gate, x)
    h = _ffn_residual(h, ffn_gain[0], f_shift, f_scale, f_gate,
                      ffn_w_gate[0].astype(BF16), ffn_w_up[0].astype(BF16), ffn_w_down[0].astype(BF16))

    kv_shift, kv_scale = parts(kv_mod, 2)
    w, plan, outs = _kv_weights(w_kv)
    kv, kv_cmp = _norm_proj(h, kv_gain, kv_shift, kv_scale, cos_t, sin_t, w, plan, outs)
    nr = s_len // CMP_STRIDE
    xc = kv_cmp.reshape(b, s_len, 2, N_KV_HEADS, HEAD_DIM)
    xc = jnp.transpose(xc, (2, 0, 3, 1, 4)).reshape(2, b, N_KV_HEADS, nr, CMP_STRIDE * HEAD_DIM)
    pe = jnp.stack([cmp_pe_k, cmp_pe_v]).reshape(2, 2, 1, CMP_STRIDE * HEAD_DIM)
    w1 = jnp.stack([cmp_w1_k, cmp_w1_v]).reshape(2, 2, CMP_STRIDE * HEAD_DIM, CMP_HIDDEN).astype(BF16)
    w2 = jnp.stack([cmp_w2_k, cmp_w2_v]).astype(BF16)
    cmp = _compress(xc, pe, w1, w2)
    cmp = jnp.transpose(cmp, (0, 1, 3, 2, 4))
    cmp = jnp.pad(cmp, ((0, 0), (0, 0), (0, 0), (0, 0), (0, LANE - HEAD_DIM)))
    kc, vc = cmp.reshape(2, b, nr, N_KV_HEADS * LANE)

    a_shift, a_scale, a_gate, f_shift, f_scale, f_gate = parts(mod[1], 6)
    w, plan, outs = _nsa_q_weights(b_w_q[0])
    q, gates = _norm_proj(h, attn_gain[1], a_shift, a_scale, cos_t, sin_t, w, plan, outs)
    o = _nsa_attention(q, gates, kc, vc, kv, _nsa_constants(s_len))
    h = _outproj_residual(o, b_w_out[0].astype(BF16), a_gate, h)
    return _moe_residual_norm(h, ffn_gain[1], f_shift, f_scale, f_gate, _pad_cols(moe_w_router[0], LANE),
                              final_gain, moe_w_gate[0].astype(BF16), moe_w_up[0].astype(BF16),
                              moe_w_down[0].astype(BF16))
```

```python
import functools

import numpy as np
import jax
import jax.numpy as jnp
from jax import lax
from jax.experimental import pallas as pl
from jax.experimental.pallas import tpu as pltpu

F32 = jnp.float32
BF16 = jnp.bfloat16

D_MODEL = 1024
HEAD_DIM = 64
N_HEADS = 16
N_KV_HEADS = 4
GROUP = N_HEADS // N_KV_HEADS
ROPE_THETA = 10000.0
RMS_EPS = 1e-6
NEG_INF = -1e30
MASKED = float("-inf")
POS_INF = 1e30
IDX_HEADS = 8
IDX_DIM = HEAD_DIM
DSA_TOPK = 256
CMP_LEN = 32
CMP_STRIDE = 16
CMP_HIDDEN = 256
SLC_LEN = 64
SLC_SHIFT = 6
SLC_TOPN = 16
WINDOW = 512
N_NSA_BRANCH = 3
N_EXPERTS = 8
TOP_K_EXPERTS = 2

LANE = 128
TQ = 256
TQI = 256
RB = 64
TK = 512
WIN_SPAN = WINDOW + TQ
TMX = 512
RBLK = 256
GRP = 512
INT_MIN = np.int32(-2 ** 31)
M_INIT = -3.0e38
Q_SCALE = HEAD_DIM ** -0.5 * float(np.log2(np.e))
VMEM_LIMIT = 56 * 1024 * 1024

_NT = (((1,), (1,)), ((), ()))


def _cparams(*sem, flags=None):
    return pltpu.CompilerParams(dimension_semantics=sem, vmem_limit_bytes=VMEM_LIMIT, flags=flags)


def _rope_kernel(pos_ref, inv_ref, sgn_ref, cos_ref, sin_ref):
    ang = pos_ref[0].astype(F32) * inv_ref[...]
    cos_ref[0] = jnp.cos(ang)
    sin_ref[0] = jnp.sin(ang) * sgn_ref[...]


def _rope_tables(positions):
    b, s = positions.shape
    inv = 1.0 / (ROPE_THETA ** (jnp.arange(0, HEAD_DIM, 2, dtype=F32) / HEAD_DIM))
    inv = jnp.tile(inv, LANE // (HEAD_DIM // 2))[None]
    sgn = np.where((np.arange(LANE) % HEAD_DIM) < HEAD_DIM // 2, -1.0, 1.0).astype(np.float32)[None]
    return pl.pallas_call(
        _rope_kernel,
        grid=(b,),
        in_specs=[pl.BlockSpec((1, s, 1), lambda i: (i, 0, 0)),
                  pl.BlockSpec((1, LANE), lambda i: (0, 0)),
                  pl.BlockSpec((1, LANE), lambda i: (0, 0))],
        out_specs=[pl.BlockSpec((1, s, LANE), lambda i: (i, 0, 0))] * 2,
        out_shape=[jax.ShapeDtypeStruct((b, s, LANE), F32)] * 2,
        compiler_params=_cparams("arbitrary"),
        name="rope_tables",
    )(positions[..., None], inv, jnp.asarray(sgn))


def _ada_kernel(c_ref, w_ref, b_ref, o_ref):
    c = c_ref[...]
    ca = c * jax.nn.sigmoid(c)
    o_ref[0] = jnp.dot(ca, w_ref[0], preferred_element_type=F32,
                       precision=lax.Precision.HIGHEST) + b_ref[0]


def _ada(c_pad, w, bias):
    nl, d, n = w.shape
    tn = 1024
    return pl.pallas_call(
        _ada_kernel,
        grid=(nl, n // tn),
        in_specs=[pl.BlockSpec((8, d), lambda l, j: (0, 0)),
                  pl.BlockSpec((1, d, tn), lambda l, j: (l, 0, j)),
                  pl.BlockSpec((1, 1, tn), lambda l, j: (l, 0, j))],
        out_specs=pl.BlockSpec((1, 8, tn), lambda l, j: (l, 0, j)),
        out_shape=jax.ShapeDtypeStruct((nl, 8, n), F32),
        compiler_params=_cparams("arbitrary", "arbitrary"),
        name="ada_mod",
    )(c_pad, w, bias.reshape(nl, 1, n))


def _norm_mod(x, g, shift, scale):
    ms = jnp.mean(x * x, axis=-1, keepdims=True)
    u = (x * lax.rsqrt(ms + RMS_EPS)) * g
    return u * (1.0 + scale) + shift


def _proj_kernel(h_ref, g_ref, sh_ref, sc_ref, cos_ref, sin_ref, w_ref, *out_refs, plan):
    ub = _norm_mod(h_ref[0], g_ref[...], sh_ref[0], sc_ref[0]).astype(BF16)
    cos = cos_ref[0]
    sin = sin_ref[0]
    lane = lax.broadcasted_iota(jnp.int32, cos.shape, 1)
    first = (lane & (HEAD_DIM - 1)) < HEAD_DIM // 2
    for (ws, width, rope, scale, oi, os_, act) in plan:
        for c0 in range(0, width, 256):
            cw = min(256, width - c0)
            y = jnp.dot(ub, w_ref[:, ws + c0: ws + c0 + cw], preferred_element_type=F32)
            for j in range(cw // LANE):
                blk = y[:, j * LANE:(j + 1) * LANE]
                if rope:
                    sw = jnp.where(first, pltpu.roll(blk, LANE - HEAD_DIM // 2, 1),
                                   pltpu.roll(blk, HEAD_DIM // 2, 1))
                    blk = blk * cos + sw * sin
                if scale != 1.0:
                    blk = blk * scale
                if act == "sigmoid":
                    blk = jax.nn.sigmoid(blk)
                elif act == "ones":
                    blk = jnp.where(lane == HEAD_DIM, 1.0, blk)
                elif act == "blockhot":
                    tok = pl.program_id(1) * cos.shape[0] + lax.broadcasted_iota(jnp.int32, cos.shape, 0)
                    blk = jnp.where(lane - HEAD_DIM == (tok >> SLC_SHIFT), 1.0, blk)
                if act == "spread":
                    o0 = os_ + 2 * (c0 + j * LANE)
                    low = lane < HEAD_DIM
                    out_refs[oi][0, :, o0:o0 + LANE] = jnp.where(low, blk, 0.0).astype(out_refs[oi].dtype)
                    out_refs[oi][0, :, o0 + LANE:o0 + 2 * LANE] = (
                        jnp.where(low, pltpu.roll(blk, HEAD_DIM, 1), 0.0).astype(out_refs[oi].dtype))
                    continue
                o0 = os_ + c0 + j * LANE
                out_refs[oi][0, :, o0:o0 + LANE] = blk.astype(out_refs[oi].dtype)


def _norm_proj(h, gain, shift, scale, cos_t, sin_t, w, plan, outs, tm=512):
    b, s, d = h.shape
    n = w.shape[1]
    vec = pl.BlockSpec((1, 1, d), lambda bi, i: (bi, 0, 0))
    return pl.pallas_call(
        functools.partial(_proj_kernel, plan=tuple(plan)),
        grid=(b, s // tm),
        in_specs=[pl.BlockSpec((1, tm, d), lambda bi, i: (bi, i, 0)),
                  pl.BlockSpec((1, d), lambda bi, i: (0, 0)),
                  vec, vec,
                  pl.BlockSpec((1, tm, LANE), lambda bi, i: (bi, i, 0)),
                  pl.BlockSpec((1, tm, LANE), lambda bi, i: (bi, i, 0)),
                  pl.BlockSpec((d, n), lambda bi, i: (0, 0))],
        out_specs=[pl.BlockSpec((1, tm, wd), lambda bi, i: (bi, i, 0)) for wd, _ in outs],
        out_shape=[jax.ShapeDtypeStruct((b, s, wd), dt) for wd, dt in outs],
        compiler_params=_cparams("arbitrary", "arbitrary"),
        name="norm_proj",
    )(h, gain.reshape(1, d), shift, scale, cos_t, sin_t, w)


def _ordered_float(u):
    key = u ^ INT_MIN
    return lax.bitcast_convert_type(key ^ ((key >> 31) & np.int32(0x7FFFFFFF)), F32)


def _indexer_kernel(iq_ref, ik_ref, iw_ref, tri_ref, bias_ref, sc_scr, iqs_scr, iwb_scr, *, n_keep):
    i = pl.program_id(1)
    q0 = i * TQI
    s_len = bias_ref.shape[2]
    n_all = s_len // TK
    nch = (q0 + TQI + TK - 1) // TK
    t = q0 + lax.broadcasted_iota(jnp.int32, (TQI, 1), 0)
    iw = iw_ref[0]
    for hh in range(IDX_HEADS):
        iqs_scr[hh * TQI:(hh + 1) * TQI, :] = iq_ref[0, :, hh * LANE:(hh + 1) * LANE]
        iwb_scr[hh] = jnp.broadcast_to(iw[:, hh:hh + 1], (TQI, LANE))

    def lanes_of(c, j):
        return pl.ds(pl.multiple_of(c * TK + j * LANE, LANE), LANE)

    def score_chunk(c, carry):
        r = lax.dot_general(iqs_scr[...], ik_ref[0, pl.ds(pl.multiple_of(c * TK, TK), TK), :], _NT,
                            preferred_element_type=F32)
        pos = c * TK + lax.broadcasted_iota(jnp.int32, (1, LANE), 1)
        for j in range(TK // LANE):
            acc = jnp.zeros((TQI, LANE), F32)
            for hh in range(IDX_HEADS):
                acc = acc + iwb_scr[hh] * jnp.maximum(r[hh * TQI:(hh + 1) * TQI, j * LANE:(j + 1) * LANE], 0.0)
            sc_scr[:, lanes_of(c, j)] = jnp.where(pos + j * LANE <= t, acc, MASKED)
        return carry

    lax.fori_loop(0, nch, score_chunk, 0)

    def lane_sum(x):
        return jnp.sum(x, axis=1, keepdims=True)

    assert n_all * (TK // LANE) < 256
    b8, b16 = np.int32(1 << 8), np.int32(1 << 16)
    blocks = list(range(0, TQI, RB))
    thr_u = [jnp.zeros((RB, 1), jnp.int32) for _ in blocks]
    cnt_thr = [jnp.zeros((RB, 1), F32) + (nch * TK).astype(F32) for _ in blocks]
    for bit in range(30, -1, -2):
        cus = [[thr_u[bi] | np.array([m << bit], np.uint32).view(np.int32)[0] for m in (1, 2, 3)]
               for bi in range(len(blocks))]
        cbs = [[jnp.broadcast_to(_ordered_float(u), (RB, LANE)) for u in cu] for cu in cus]
        parts = []
        for bi, rb in enumerate(blocks):
            def body(c, part, cb=cbs[bi], rb=rb):
                for j in range(TK // LANE):
                    x = sc_scr[rb:rb + RB, lanes_of(c, j)]
                    part = part + jnp.where(x >= cb[2], 1 + b8 + b16,
                                            jnp.where(x >= cb[1], 1 + b8, jnp.where(x >= cb[0], 1, 0)))
                return part

            parts.append(lax.fori_loop(0, nch, body, jnp.zeros((RB, LANE), jnp.int32)))
        for bi in range(len(blocks)):
            for m, sh in enumerate((0, 8, 16)):
                cnt = lane_sum(((parts[bi] >> sh) & 0xFF).astype(F32))
                ok = cnt >= n_keep
                thr_u[bi] = jnp.where(ok, cus[bi][m], thr_u[bi])
                cnt_thr[bi] = jnp.where(ok, cnt, cnt_thr[bi])

    thr, has_thr, tied = [], [], None
    for bi in range(len(blocks)):
        f = _ordered_float(thr_u[bi])
        has = f > MASKED
        thr.append(jnp.where(has, f, MASKED))
        has_thr.append(has)
        any_tie = jnp.max(jnp.where(has & (cnt_thr[bi] > n_keep), 1, 0)) > 0
        tied = any_tie if tied is None else jnp.logical_or(tied, any_tie)

    @pl.when(jnp.logical_not(tied))
    def _():
        for bi, rb in enumerate(blocks):
            thr_b = jnp.broadcast_to(thr[bi], (RB, LANE))

            def body(c, carry, thr_b=thr_b, rb=rb):
                for j in range(TK // LANE):
                    x = sc_scr[rb:rb + RB, lanes_of(c, j)]
                    bias_ref[0, rb:rb + RB, lanes_of(c, j)] = (
                        jnp.where((x >= thr_b) & (x > MASKED), 0.0, NEG_INF).astype(BF16))
                return carry
            lax.fori_loop(0, nch, body, 0)

    @pl.when(tied)
    def _():
        for bi, rb in enumerate(blocks):
            th, has = thr[bi], has_thr[bi]

            def count_gt(c, part, th=th, rb=rb):
                x = sc_scr[rb:rb + RB, pl.ds(pl.multiple_of(c * TK, TK), TK)]
                return part + lane_sum(jnp.where(x > th, 1.0, 0.0))
            need = n_keep - lax.fori_loop(0, nch, count_gt, jnp.zeros((RB, 1), F32))

            def body(c, seen, th=th, has=has, need=need, rb=rb):
                x = sc_scr[rb:rb + RB, pl.ds(pl.multiple_of(c * TK, TK), TK)]
                tie = x == th
                pref = jnp.dot(jnp.where(tie, 1.0, 0.0).astype(BF16), tri_ref[...],
                               preferred_element_type=F32) + seen
                sel = ((x > th) | (tie & (pref <= need) & has)) & (x > MASKED)
                bias_ref[0, rb:rb + RB, pl.ds(pl.multiple_of(c * TK, TK), TK)] = (
                    jnp.where(sel, 0.0, NEG_INF).astype(BF16))
                return pref[:, TK - 1:TK]
            lax.fori_loop(0, nch, body, jnp.zeros((RB, 1), F32))

    def fill(c, carry):
        bias_ref[0, :, pl.ds(pl.multiple_of(c * TK, TK), TK)] = jnp.full((TQI, TK), NEG_INF, BF16)
        return carry
    lax.fori_loop(nch, n_all, fill, 0)


def _dsa_bias(proj, iw, s_len):
    b = proj.shape[0]
    n_keep = min(DSA_TOPK, s_len // 4)
    tri = jnp.asarray(np.triu(np.ones((TK, TK), np.float32)), BF16)
    return pl.pallas_call(
        functools.partial(_indexer_kernel, n_keep=n_keep),
        grid=(b, s_len // TQI),
        in_specs=[pl.BlockSpec((1, TQI, IDX_HEADS * LANE), lambda bi, i: (bi, i, 2)),
                  pl.BlockSpec((1, s_len, LANE), lambda bi, i: (bi, 0, 32)),
                  pl.BlockSpec((1, TQI, LANE), lambda bi, i: (bi, i, 0)),
                  pl.BlockSpec((TK, TK), lambda bi, i: (0, 0))],
        out_specs=pl.BlockSpec((1, TQI, s_len), lambda bi, i: (bi, i, 0)),
        out_shape=jax.ShapeDtypeStruct((b, s_len, s_len), BF16),
        scratch_shapes=[pltpu.VMEM((TQI, s_len), F32),
                        pltpu.VMEM((IDX_HEADS * TQI, LANE), BF16),
                        pltpu.VMEM((IDX_HEADS, TQI, LANE), F32)],
        compiler_params=_cparams("arbitrary", "arbitrary"),
        name="dsa_indexer",
    )(proj, proj, iw, tri)


def _stack_heads(q_ref, qs_scr):
    for g in range(N_KV_HEADS):
        for r in range(GROUP):
            hh = g * GROUP + r
            qs_scr[g, r * TQ:(r + 1) * TQ, :] = q_ref[0, :, hh * LANE:(hh + 1) * LANE]


def _online_update(s, v, m_scr, acc_scr, g):
    m_old = m_scr[g]
    m_new = jnp.maximum(m_old, jnp.max(s, axis=1, keepdims=True))
    alpha = jnp.exp2(m_old - m_new)
    p = jnp.exp2(s - jnp.concatenate([m_new] * (s.shape[1] // LANE), axis=1))
    acc_scr[g] = alpha * acc_scr[g] + jnp.dot(p.astype(BF16), v, preferred_element_type=F32)
    m_scr[g] = m_new


def _init_online(m_scr, acc_scr):
    m_scr[...] = jnp.full(m_scr.shape, M_INIT, F32)
    acc_scr[...] = jnp.zeros(acc_scr.shape, F32)


def _normalised(acc):
    return acc / acc[:, HEAD_DIM:HEAD_DIM + 1]


def _head(g):
    return slice(g * LANE, (g + 1) * LANE)


def _out_rows(w_out):
    d = w_out.shape[1]
    w = w_out.reshape(N_HEADS, HEAD_DIM, d)
    return jnp.pad(w, ((0, 0), (0, LANE - HEAD_DIM), (0, 0))).reshape(N_HEADS * LANE, d)


def _dsa_attn_kernel(q_ref, k_ref, v_ref, bias_ref, o_ref, qs_scr, m_scr, acc_scr):
    i = pl.program_id(1)
    nch = (i * TQ + TQ + TK - 1) // TK
    _stack_heads(q_ref, qs_scr)
    _init_online(m_scr, acc_scr)

    def chunk(c, carry):
        k0 = pl.multiple_of(c * TK, TK)
        bias = bias_ref[0, :, pl.ds(k0, TK)].astype(F32)
        bias4 = jnp.concatenate([bias] * GROUP, axis=0)
        for g in range(N_KV_HEADS):
            s = lax.dot_general(qs_scr[g], k_ref[0, pl.ds(k0, TK), _head(g)], _NT,
                                preferred_element_type=F32) + bias4
            _online_update(s, v_ref[0, pl.ds(k0, TK), _head(g)], m_scr, acc_scr, g)
        return carry

    lax.fori_loop(0, nch, chunk, 0)
    for g in range(N_KV_HEADS):
        og = _normalised(acc_scr[g])
        for r in range(GROUP):
            hh = g * GROUP + r
            o_ref[0, :, hh * LANE:(hh + 1) * LANE] = og[r * TQ:(r + 1) * TQ].astype(BF16)


def _dsa_attention(proj, bias):
    b, s_len, _ = proj.shape
    return pl.pallas_call(
        _dsa_attn_kernel,
        grid=(b, s_len // TQ),
        in_specs=[pl.BlockSpec((1, TQ, N_HEADS * LANE), lambda bi, i: (bi, i, 0)),
                  pl.BlockSpec((1, s_len, N_KV_HEADS * LANE), lambda bi, i: (bi, 0, 6)),
                  pl.BlockSpec((1, s_len, N_KV_HEADS * LANE), lambda bi, i: (bi, 0, 7)),
                  pl.BlockSpec((1, TQ, s_len), lambda bi, i: (bi, i, 0))],
        out_specs=pl.BlockSpec((1, TQ, N_HEADS * LANE), lambda bi, i: (bi, i, 0)),
        out_shape=jax.ShapeDtypeStruct((b, s_len, N_HEADS * LANE), BF16),
        scratch_shapes=[pltpu.VMEM((N_KV_HEADS, GROUP * TQ, LANE), BF16),
                        pltpu.VMEM((N_KV_HEADS, GROUP * TQ, LANE), F32),
                        pltpu.VMEM((N_KV_HEADS, GROUP * TQ, LANE), F32)],
        compiler_params=_cparams("arbitrary", "arbitrary"),
        name="dsa_attention",
    )(proj, proj, proj, bias)


def _outproj_kernel(o_ref, w_ref, gate_ref, h_ref, out_ref):
    y = jnp.dot(o_ref[0], w_ref[...], preferred_element_type=F32)
    out_ref[0] = h_ref[0] + gate_ref[0] * y


def _outproj_residual(o, w, gate, h, tm=512):
    b, s, d = h.shape
    k = o.shape[2]
    return pl.pallas_call(
        _outproj_kernel,
        grid=(b, s // tm),
        in_specs=[pl.BlockSpec((1, tm, k), lambda bi, i: (bi, i, 0)),
                  pl.BlockSpec((k, d), lambda bi, i: (0, 0)),
                  pl.BlockSpec((1, 1, d), lambda bi, i: (bi, 0, 0)),
                  pl.BlockSpec((1, tm, d), lambda bi, i: (bi, i, 0))],
        out_specs=pl.BlockSpec((1, tm, d), lambda bi, i: (bi, i, 0)),
        out_shape=jax.ShapeDtypeStruct((b, s, d), F32),
        compiler_params=_cparams("arbitrary", "arbitrary"),
        name="outproj_residual",
    )(o, w, gate, h)


def _ffn_kernel(h_ref, g_ref, sh_ref, sc_ref, gate_ref, wg_ref, wu_ref, wd_ref, out_ref, u_scr, acc_scr):
    f = pl.program_id(2)

    @pl.when(f == 0)
    def _():
        u_scr[...] = _norm_mod(h_ref[0], g_ref[...], sh_ref[0], sc_ref[0]).astype(BF16)
        acc_scr[...] = jnp.zeros(acc_scr.shape, F32)

    ub = u_scr[...]
    a = jnp.dot(ub, wg_ref[...], preferred_element_type=F32)
    up = jnp.dot(ub, wu_ref[...], preferred_element_type=F32)
    hid = (a * jax.nn.sigmoid(a)) * up
    acc_scr[...] += jnp.dot(hid.astype(BF16), wd_ref[...], preferred_element_type=F32)

    @pl.when(f == pl.num_programs(2) - 1)
    def _():
        out_ref[0] = h_ref[0] + gate_ref[0] * acc_scr[...]


def _ffn_residual(h, gain, shift, scale, gate, wg, wu, wd, tm=512):
    b, s, d = h.shape
    dff = wg.shape[1]
    tf = dff
    vec = pl.BlockSpec((1, 1, d), lambda bi, i, f: (bi, 0, 0))
    once = pl.Buffered(1)
    return pl.pallas_call(
        _ffn_kernel,
        grid=(b, s // tm, dff // tf),
        in_specs=[pl.BlockSpec((1, tm, d), lambda bi, i, f: (bi, i, 0)),
                  pl.BlockSpec((1, d), lambda bi, i, f: (0, 0)),
                  vec, vec, vec,
                  pl.BlockSpec((d, tf), lambda bi, i, f: (0, f), pipeline_mode=once),
                  pl.BlockSpec((d, tf), lambda bi, i, f: (0, f), pipeline_mode=once),
                  pl.BlockSpec((tf, d), lambda bi, i, f: (f, 0), pipeline_mode=once)],
        out_specs=pl.BlockSpec((1, tm, d), lambda bi, i, f: (bi, i, 0)),
        out_shape=jax.ShapeDtypeStruct((b, s, d), F32),
        scratch_shapes=[pltpu.VMEM((tm, d), BF16), pltpu.VMEM((tm, d), F32)],
        compiler_params=_cparams("arbitrary", "arbitrary", "arbitrary"),
        name="ffn_residual",
    )(h, gain.reshape(1, d), shift, scale, gate, wg, wu, wd)


def _compress_kernel(x_ref, pe_ref, w1_ref, w2_ref, o_ref):
    x = x_ref[0, 0, 0]
    nr = x.shape[0]
    pa = jnp.dot((x + pe_ref[0, 0]).astype(BF16), w1_ref[0, 0], preferred_element_type=F32)
    pb = jnp.dot((x + pe_ref[0, 1]).astype(BF16), w1_ref[0, 1], preferred_element_type=F32)
    hid = pa + pltpu.roll(pb, nr - 1, 0)
    act = 0.5 * hid * (1.0 + jnp.tanh(np.float32(np.sqrt(2.0 / np.pi)) * (hid + 0.044715 * (hid * hid * hid))))
    o_ref[0, 0, 0] = jnp.dot(act.astype(BF16), w2_ref[0], preferred_element_type=F32).astype(BF16)


def _compress(x, pe, w1, w2):
    two, b, g, nr, kk = x.shape
    return pl.pallas_call(
        _compress_kernel,
        grid=(two, b, g),
        in_specs=[pl.BlockSpec((1, 1, 1, nr, kk), lambda a, bi, gi: (a, bi, gi, 0, 0)),
                  pl.BlockSpec((1, 2, 1, kk), lambda a, bi, gi: (a, 0, 0, 0)),
                  pl.BlockSpec((1, 2, kk, CMP_HIDDEN), lambda a, bi, gi: (a, 0, 0, 0)),
                  pl.BlockSpec((1, CMP_HIDDEN, HEAD_DIM), lambda a, bi, gi: (a, 0, 0))],
        out_specs=pl.BlockSpec((1, 1, 1, nr, HEAD_DIM), lambda a, bi, gi: (a, bi, gi, 0, 0)),
        out_shape=jax.ShapeDtypeStruct((two, b, g, nr, HEAD_DIM), BF16),
        compiler_params=_cparams("arbitrary", "arbitrary", "arbitrary"),
        name="nsa_compress",
    )(x, pe, w1, w2)


def _nsa_kernel(q_ref, gt_ref, kc_ref, vc_ref, ks_ref, kw_ref, vs_ref, vw_ref, aggt_ref, o_ref,
                qs_scr, oc_scr, m_scr, acc_scr, *, n_sel):
    i = pl.program_id(1)
    q0 = i * TQ
    nch = (q0 + TQ + TK - 1) // TK
    _stack_heads(q_ref, qs_scr)
    _init_online(m_scr, acc_scr)
    t = q0 + lax.broadcasted_iota(jnp.int32, (TQ, 1), 0)
    t4 = jnp.concatenate([t] * GROUP, axis=0)

    n_cmp = kc_ref.shape[1]
    cmp_end = lax.broadcasted_iota(jnp.int32, (1, n_cmp), 1) * CMP_STRIDE + (CMP_LEN - 1)
    cmask = cmp_end <= t4
    n_blk = aggt_ref.shape[0]
    jrow = lax.broadcasted_iota(jnp.int32, (n_blk, TQ), 0)
    tq = q0 + lax.broadcasted_iota(jnp.int32, (n_blk, TQ), 1)
    jt = tq >> SLC_SHIFT
    valid = jrow * SLC_LEN <= tq
    forced = (jrow == 0) | (jrow == jt) | (jrow == jt - 1)
    for g in range(N_KV_HEADS):
        sc = lax.dot_general(qs_scr[g], kc_ref[0, :, g * LANE:(g + 1) * LANE], _NT, preferred_element_type=F32)
        sm = jnp.where(cmask, sc, NEG_INF)
        e = jnp.where(cmask, jnp.exp2(sm - jnp.max(sm, axis=1, keepdims=True)), 0.0)
        pc = e / jnp.maximum(jnp.sum(e, axis=1, keepdims=True), 1e-30)
        oc_scr[g] = jnp.dot(pc.astype(BF16), vc_ref[0, :, _head(g)], preferred_element_type=F32)
        pcs = pc[0:TQ]
        for r in range(1, GROUP):
            pcs = pcs + pc[r * TQ:(r + 1) * TQ]
        imp = lax.dot_general(aggt_ref[...], pcs, _NT, preferred_element_type=F32,
                              precision=lax.Precision.HIGHEST)
        imp = jnp.where(valid, jnp.where(forced, POS_INF, imp), NEG_INF)
        rank = jnp.zeros((n_blk, TQ), jnp.int32)
        for j in range(n_blk):
            row = imp[j:j + 1, :]
            beats = (row > imp) | ((row == imp) & (jrow > j))
            rank = rank + beats.astype(jnp.int32)
        sel = (rank < n_sel) & valid
        selb = jnp.where(sel, 0.0, NEG_INF)
        pads = [jnp.zeros((HEAD_DIM, TQ), F32), selb]
        if n_blk < LANE - HEAD_DIM:
            pads.append(jnp.zeros((LANE - HEAD_DIM - n_blk, TQ), F32))
        selb = jnp.concatenate(pads, axis=0)
        selb = selb.T.astype(BF16)
        qs_scr[g] = qs_scr[g] + jnp.concatenate([selb] * GROUP, axis=0)

    def slc_chunk(c, diagonal):
        k0 = pl.multiple_of(c * TK, TK)
        for g in range(N_KV_HEADS):
            s = lax.dot_general(qs_scr[g], ks_ref[0, pl.ds(k0, TK), _head(g)], _NT,
                                preferred_element_type=F32)
            if diagonal:
                pos = k0 + lax.broadcasted_iota(jnp.int32, (1, TK), 1)
                s = jnp.where(pos <= t4, s, NEG_INF)
            _online_update(s, vs_ref[0, pl.ds(k0, TK), _head(g)], m_scr, acc_scr, g)

    def slc_body(c, carry):
        slc_chunk(c, False)
        return carry

    lax.fori_loop(0, nch - 1, slc_body, 0)
    slc_chunk(nch - 1, True)

    gt = gt_ref[0]
    w0 = pl.multiple_of(jnp.maximum(q0 - WINDOW, 0), TQ)
    wpos = w0 + lax.broadcasted_iota(jnp.int32, (1, WIN_SPAN), 1)
    wmask = (wpos <= t4) & (wpos > t4 - WINDOW)
    for g in range(N_KV_HEADS):
        s = lax.dot_general(qs_scr[g], kw_ref[0, pl.ds(w0, WIN_SPAN), _head(g)], _NT,
                            preferred_element_type=F32)
        s = jnp.where(wmask, s, NEG_INF)
        p = jnp.exp2(s - jnp.max(s, axis=1, keepdims=True))
        p = jnp.where(wmask, p, 0.0)
        ow = _normalised(jnp.dot(p.astype(BF16), vw_ref[0, pl.ds(w0, WIN_SPAN), _head(g)],
                                 preferred_element_type=F32))
        osl = _normalised(acc_scr[g])
        oc = oc_scr[g]
        for r in range(GROUP):
            hh = g * GROUP + r
            rs = slice(r * TQ, (r + 1) * TQ)
            o = (gt[:, hh:hh + 1] * oc[rs] + gt[:, N_HEADS + hh:N_HEADS + hh + 1] * osl[rs]
                 + gt[:, 2 * N_HEADS + hh:2 * N_HEADS + hh + 1] * ow[rs])
            o_ref[0, :, hh * LANE:(hh + 1) * LANE] = o.astype(BF16)


def _nsa_attention(q, gates, kc, vc, kv, agg_t):
    b, s_len, _ = q.shape
    n_cmp = kc.shape[1]
    n_blk = agg_t.shape[0]
    n_sel = min(SLC_TOPN, s_len // SLC_LEN)
    rows = GROUP * TQ
    return pl.pallas_call(
        functools.partial(_nsa_kernel, n_sel=n_sel),
        grid=(b, s_len // TQ),
        in_specs=[pl.BlockSpec((1, TQ, N_HEADS * LANE), lambda bi, i: (bi, i, 0)),
                  pl.BlockSpec((1, TQ, LANE), lambda bi, i: (bi, i, 0)),
                  pl.BlockSpec((1, n_cmp, N_KV_HEADS * LANE), lambda bi, i: (bi, 0, 0)),
                  pl.BlockSpec((1, n_cmp, N_KV_HEADS * LANE), lambda bi, i: (bi, 0, 0)),
                  pl.BlockSpec((1, s_len, N_KV_HEADS * LANE), lambda bi, i: (bi, 0, 0), pipeline_mode=pl.Buffered(1)),
                  pl.BlockSpec((1, s_len, N_KV_HEADS * LANE), lambda bi, i: (bi, 0, 1), pipeline_mode=pl.Buffered(1)),
                  pl.BlockSpec((1, s_len, N_KV_HEADS * LANE), lambda bi, i: (bi, 0, 2), pipeline_mode=pl.Buffered(1)),
                  pl.BlockSpec((1, s_len, N_KV_HEADS * LANE), lambda bi, i: (bi, 0, 3), pipeline_mode=pl.Buffered(1)),
                  pl.BlockSpec((n_blk, n_cmp), lambda bi, i: (0, 0))],
        out_specs=pl.BlockSpec((1, TQ, N_HEADS * LANE), lambda bi, i: (bi, i, 0)),
        out_shape=jax.ShapeDtypeStruct((b, s_len, N_HEADS * LANE), BF16),
        scratch_shapes=[pltpu.VMEM((N_KV_HEADS, rows, LANE), BF16),
                        pltpu.VMEM((N_KV_HEADS, rows, LANE), F32),
                        pltpu.VMEM((N_KV_HEADS, rows, LANE), F32),
                        pltpu.VMEM((N_KV_HEADS, rows, LANE), F32)],
        compiler_params=_cparams("arbitrary", "arbitrary"),
        name="nsa_attention",
    )(q, gates, kc, vc, kv, kv, kv, kv, agg_t)


def _router_kernel(h_ref, g_ref, sh_ref, sc_ref, wr_ref, low_ref, u_ref, info_ref, cnt_ref, carry_scr):
    @pl.when((pl.program_id(0) == 0) & (pl.program_id(1) == 0))
    def _():
        carry_scr[...] = jnp.zeros(carry_scr.shape, F32)

    u = _norm_mod(h_ref[0], g_ref[...], sh_ref[0], sc_ref[0])
    u_ref[0] = u.astype(BF16)
    logits = jnp.dot(u, wr_ref[...], preferred_element_type=F32, precision=lax.Precision.HIGHEST)
    lane = lax.broadcasted_iota(jnp.int32, logits.shape, 1)
    logits = jnp.where(lane < N_EXPERTS, logits, -jnp.inf)
    m1 = jnp.max(logits, axis=1, keepdims=True)
    i1 = jnp.min(jnp.where(logits == m1, lane, LANE), axis=1, keepdims=True)
    rest = jnp.where(lane == i1, -jnp.inf, logits)
    m2 = jnp.max(rest, axis=1, keepdims=True)
    i2 = jnp.min(jnp.where(rest == m2, lane, LANE), axis=1, keepdims=True)
    e2 = jnp.exp(m2 - m1)
    w1 = 1.0 / (1.0 + e2)
    w2 = e2 / (1.0 + e2)
    hit1 = lane == i1
    hit2 = lane == i2
    onehot = jnp.where(hit1 | hit2, 1.0, 0.0).astype(BF16)
    ahead = jnp.dot(low_ref[...], onehot, preferred_element_type=F32) + carry_scr[0:1, :]
    pos1 = jnp.sum(jnp.where(hit1, ahead, 0.0), axis=1, keepdims=True)
    pos2 = jnp.sum(jnp.where(hit2, ahead, 0.0), axis=1, keepdims=True)
    tile_cnt = jnp.dot(jnp.ones((8, TMX), BF16), onehot, preferred_element_type=F32)
    cnt_ref[0] = tile_cnt
    carry_scr[...] += tile_cnt
    info = jnp.zeros(logits.shape, F32)
    for j, v in enumerate((i1.astype(F32), i2.astype(F32), pos1, pos2, w1, w2)):
        info = jnp.where(lane == j, v, info)
    info_ref[0] = info


def _moe_gather_kernel(gb_ref, gs_ref, u_ref, dr_ref, w_ref, xs_ref, ws_ref):
    d = pl.program_id(0)
    xs_ref[...] = jnp.zeros(xs_ref.shape, BF16)
    ws_ref[...] = jnp.zeros(ws_ref.shape, F32)
    row = d * RBLK + lax.broadcasted_iota(jnp.int32, (RBLK, TMX), 0)

    def pair(k, carry):
        s = gs_ref[k]
        p1 = dr_ref[s, 0:1, :] == row
        p2 = dr_ref[s, 1:2, :] == row
        onehot = jnp.where(p1 | p2, 1.0, 0.0).astype(BF16)
        tok = u_ref[pl.ds(pl.multiple_of(s * TMX, TMX), TMX), :]
        xs_ref[...] += jnp.dot(onehot, tok, preferred_element_type=F32).astype(BF16)
        wrow = jnp.where(p1, w_ref[s, 0:1, :], 0.0) + jnp.where(p2, w_ref[s, 1:2, :], 0.0)
        ws_ref[...] += jnp.sum(wrow, axis=1, keepdims=True)
        return carry

    lax.fori_loop(gb_ref[d], gb_ref[d + 1], pair, 0)


def _moe_ffn_kernel(te_ref, nt_ref, x_ref, ws_ref, wg_ref, wu_ref, wd_ref, o_ref, acc_scr):
    i = pl.program_id(0)
    f = pl.program_id(1)
    live = i < nt_ref[0]

    @pl.when(f == 0)
    def _():
        acc_scr[...] = jnp.zeros(acc_scr.shape, F32)

    @pl.when(live)
    def _():
        x = x_ref[...]
        a = jnp.dot(x, wg_ref[0], preferred_element_type=F32)
        up = jnp.dot(x, wu_ref[0], preferred_element_type=F32)
        hid = (a * jax.nn.sigmoid(a)) * up
        acc_scr[...] += jnp.dot(hid.astype(BF16), wd_ref[0], preferred_element_type=F32)

    @pl.when(f == pl.num_programs(1) - 1)
    def _():
        w = jnp.concatenate([ws_ref[...]] * (acc_scr.shape[1] // LANE), axis=1)
        o_ref[...] = jnp.where(live, acc_scr[...] * w, 0.0).astype(BF16)


def _moe_combine_kernel(ys_ref, yd_ref, nv_ref, o_ref, dr_ref, h_ref, gate_ref, fg_ref, out_ref, y_scr):
    k = pl.program_id(0)
    last_k = pl.num_programs(0) - 1
    s = ys_ref[k]
    d = yd_ref[k]

    @pl.when((k == 0) | (ys_ref[jnp.maximum(k - 1, 0)] != s))
    def _():
        y_scr[...] = jnp.zeros(y_scr.shape, F32)

    @pl.when(k < nv_ref[0])
    def _():
        col = d * RBLK + lax.broadcasted_iota(jnp.int32, (TMX, RBLK), 1)
        dr = dr_ref[...]
        onehot = jnp.where((dr[:, 0:1] == col) | (dr[:, 1:2] == col), 1.0, 0.0).astype(BF16)
        y_scr[...] += jnp.dot(onehot, o_ref[...], preferred_element_type=F32)

    @pl.when((k == last_k) | (ys_ref[jnp.minimum(k + 1, last_k)] != s))
    def _():
        hn = h_ref[0] + gate_ref[0] * y_scr[...]
        ms = jnp.mean(hn * hn, axis=-1, keepdims=True)
        out_ref[0] = (hn * lax.rsqrt(ms + RMS_EPS)) * fg_ref[...]


def _moe_plan(info, cnt, n_tok):
    nt = n_tok // TMX
    n_blk = (TOP_K_EXPERTS * n_tok + N_EXPERTS * GRP) // RBLK
    n_pairs = n_blk + N_EXPERTS * nt
    e1, e2, pos1, pos2 = [info[:, j].astype(jnp.int32) for j in range(4)]
    counts = cnt[:, 0, :N_EXPERTS].astype(jnp.int32)
    padded = (jnp.sum(counts, axis=0) + GRP - 1) // GRP * GRP
    ends = jnp.cumsum(padded)
    start = ends - padded
    dr1 = start[e1] + pos1
    dr2 = start[e2] + pos2
    first = start[None, :] + jnp.cumsum(counts, axis=0) - counts
    last = first + counts - 1
    fb = first // RBLK
    nb = jnp.where(counts > 0, last // RBLK - fb + 1, 0)
    k3 = jnp.arange(3)
    cd = (fb[..., None] + k3).reshape(-1)
    cs = jnp.broadcast_to(jnp.arange(nt)[:, None, None], (nt, N_EXPERTS, 3)).reshape(-1)
    cv = (k3 < nb[..., None]).reshape(-1)
    n_valid = jnp.sum(cv.astype(jnp.int32))

    def ordered(key):
        smaller = cv[None, :] & (key[None, :] < key[:, None])
        slot = jnp.where(cv, jnp.sum(smaller.astype(jnp.int32), axis=1), -1)
        want = jnp.minimum(jnp.arange(n_pairs), n_valid - 1)
        pick = slot[None, :] == want[:, None]
        return (jnp.sum(jnp.where(pick, cd[None, :], 0), axis=1).astype(jnp.int32),
                jnp.sum(jnp.where(pick, cs[None, :], 0), axis=1).astype(jnp.int32))

    gd, gs = ordered(cd * nt + cs)
    blocks = jnp.arange(n_blk + 1)
    gb = jnp.sum((cv[None, :] & (cd[None, :] < blocks[:, None])).astype(jnp.int32), axis=1)
    yd, ys = ordered(cs * n_blk + cd)
    n_ffn = n_blk * RBLK // GRP
    tile_row = jnp.arange(n_ffn) * GRP
    tile_e = jnp.minimum(jnp.sum((ends[None, :] <= tile_row[:, None]).astype(jnp.int32), axis=1), N_EXPERTS - 1)
    return dict(dr1=dr1, dr2=dr2, gb=gb.astype(jnp.int32), gs=gs, yd=yd, ys=ys, n_valid=n_valid.reshape(1),
                tile_e=tile_e.astype(jnp.int32), n_live=(ends[-1] // GRP).astype(jnp.int32).reshape(1),
                n_blk=n_blk, n_pairs=n_pairs, n_ffn=n_ffn)


def _moe_residual_norm(h, gain, shift, scale, gate, w_router, final_gain, wg, wu, wd, tf=1792):
    b, s, d = h.shape
    n_tok = b * s
    tpb = s // TMX
    nt = n_tok // TMX
    ne, _, dff = wg.shape
    vec = pl.BlockSpec((1, 1, d), lambda bi, i: (bi, 0, 0))
    low = jnp.asarray(np.tril(np.ones((TMX, TMX), np.float32), -1), BF16)
    u, info, cnt = pl.pallas_call(
        _router_kernel,
        grid=(b, tpb),
        in_specs=[pl.BlockSpec((1, TMX, d), lambda bi, i: (bi, i, 0)),
                  pl.BlockSpec((1, d), lambda bi, i: (0, 0)),
                  vec, vec,
                  pl.BlockSpec((d, LANE), lambda bi, i: (0, 0)),
                  pl.BlockSpec((TMX, TMX), lambda bi, i: (0, 0))],
        out_specs=[pl.BlockSpec((1, TMX, d), lambda bi, i: (bi, i, 0)),
                   pl.BlockSpec((1, TMX, LANE), lambda bi, i: (bi, i, 0)),
                   pl.BlockSpec((1, 8, LANE), lambda bi, i: (bi * tpb + i, 0, 0))],
        out_shape=[jax.ShapeDtypeStruct((b, s, d), BF16),
                   jax.ShapeDtypeStruct((b, s, LANE), F32),
                   jax.ShapeDtypeStruct((nt, 8, LANE), F32)],
        scratch_shapes=[pltpu.VMEM((8, LANE), F32)],
        compiler_params=_cparams("arbitrary", "arbitrary"),
        name="moe_router",
    )(h, gain.reshape(1, d), shift, scale, w_router, low)

    info = info.reshape(n_tok, LANE)
    plan = _moe_plan(info, cnt, n_tok)
    n_blk, n_pairs, n_ffn = plan["n_blk"], plan["n_pairs"], plan["n_ffn"]
    pad6 = jnp.zeros((nt, 6, TMX), jnp.int32)
    dr_rows = jnp.concatenate([plan["dr1"].reshape(nt, 1, TMX), plan["dr2"].reshape(nt, 1, TMX), pad6], axis=1)
    w_rows = jnp.concatenate([info[:, 4].reshape(nt, 1, TMX), info[:, 5].reshape(nt, 1, TMX),
                              pad6.astype(F32)], axis=1)
    dr_cols = jnp.pad(jnp.stack([plan["dr1"], plan["dr2"]], axis=1), ((0, 0), (0, LANE - 2)))

    xs, ws = pl.pallas_call(
        _moe_gather_kernel,
        grid_spec=pltpu.PrefetchScalarGridSpec(
            num_scalar_prefetch=2,
            grid=(n_blk,),
            in_specs=[pl.BlockSpec((n_tok, d), lambda k, gb, gs: (0, 0), pipeline_mode=pl.Buffered(1)),
                      pl.BlockSpec((nt, 8, TMX), lambda k, gb, gs: (0, 0, 0), pipeline_mode=pl.Buffered(1)),
                      pl.BlockSpec((nt, 8, TMX), lambda k, gb, gs: (0, 0, 0), pipeline_mode=pl.Buffered(1))],
            out_specs=[pl.BlockSpec((RBLK, d), lambda k, gb, gs: (k, 0)),
                       pl.BlockSpec((RBLK, LANE), lambda k, gb, gs: (k, 0))]),
        out_shape=[jax.ShapeDtypeStruct((n_blk * RBLK, d), BF16),
                   jax.ShapeDtypeStruct((n_blk * RBLK, LANE), F32)],
        compiler_params=_cparams("arbitrary"),
        name="moe_gather",
    )(plan["gb"], plan["gs"], u.reshape(n_tok, d), dr_rows, w_rows)

    def live_tile(i, te, nl):
        return jnp.maximum(jnp.minimum(i, nl[0] - 1), 0)

    rows_out = pl.pallas_call(
        _moe_ffn_kernel,
        grid_spec=pltpu.PrefetchScalarGridSpec(
            num_scalar_prefetch=2,
            grid=(n_ffn, dff // tf),
            in_specs=[pl.BlockSpec((GRP, d), lambda i, f, te, nl: (live_tile(i, te, nl), 0)),
                      pl.BlockSpec((GRP, LANE), lambda i, f, te, nl: (live_tile(i, te, nl), 0)),
                      pl.BlockSpec((1, d, tf), lambda i, f, te, nl: (te[i], 0, f)),
                      pl.BlockSpec((1, d, tf), lambda i, f, te, nl: (te[i], 0, f)),
                      pl.BlockSpec((1, tf, d), lambda i, f, te, nl: (te[i], f, 0))],
            out_specs=pl.BlockSpec((GRP, d), lambda i, f, te, nl: (i, 0)),
            scratch_shapes=[pltpu.VMEM((GRP, d), F32)]),
        out_shape=jax.ShapeDtypeStruct((n_blk * RBLK, d), BF16),
        compiler_params=_cparams("arbitrary", "arbitrary"),
        name="moe_experts",
    )(plan["tile_e"], plan["n_live"], xs, ws, wg, wu, wd)

    return pl.pallas_call(
        _moe_combine_kernel,
        grid_spec=pltpu.PrefetchScalarGridSpec(
            num_scalar_prefetch=3,
            grid=(n_pairs,),
            in_specs=[pl.BlockSpec((RBLK, d), lambda k, ys, yd, nv: (yd[k], 0)),
                      pl.BlockSpec((TMX, LANE), lambda k, ys, yd, nv: (ys[k], 0)),
                      pl.BlockSpec((1, TMX, d), lambda k, ys, yd, nv: (ys[k] // tpb, ys[k] % tpb, 0)),
                      pl.BlockSpec((1, 1, d), lambda k, ys, yd, nv: (ys[k] // tpb, 0, 0)),
                      pl.BlockSpec((1, d), lambda k, ys, yd, nv: (0, 0))],
            out_specs=pl.BlockSpec((1, TMX, d), lambda k, ys, yd, nv: (ys[k] // tpb, ys[k] % tpb, 0)),
            scratch_shapes=[pltpu.VMEM((TMX, d), F32)]),
        out_shape=jax.ShapeDtypeStruct((b, s, d), F32),
        compiler_params=_cparams("arbitrary"),
        name="moe_combine_norm",
    )(plan["ys"], plan["yd"], plan["n_valid"], rows_out, dr_cols, h, gate, final_gain.reshape(1, d))


def _pad_heads(w, n_heads):
    d = w.shape[0]
    w = w.reshape(d, n_heads, HEAD_DIM)
    return jnp.pad(w, ((0, 0), (0, 0), (0, LANE - HEAD_DIM))).reshape(d, n_heads * LANE)


def _pad_cols(w, n):
    return jnp.pad(w, ((0, 0), (0, n - w.shape[1])))


def _dsa_weights(w_in):
    q, k, v, iq, ik, iw = jnp.split(w_in, [1024, 1280, 1536, 2048, 2112], axis=1)
    w = jnp.concatenate([q, iq, _pad_heads(k, N_KV_HEADS), _pad_heads(v, N_KV_HEADS),
                         _pad_cols(ik, LANE), _pad_cols(iw, LANE)], axis=1).astype(BF16)
    plan = [(0, 1024, True, Q_SCALE, 0, 0, "spread"),
            (1024, 512, True, 1.0, 0, 2048, "spread"),
            (1536, 512, True, 1.0, 0, 3072, None),
            (2048, 512, False, 1.0, 0, 3584, "ones"),
            (2560, 128, True, 1.0, 0, 4096, None),
            (2688, 128, False, IDX_HEADS ** -0.5 * IDX_DIM ** -0.5, 1, 0, None)]
    return w, plan, [(4224, BF16), (LANE, F32)]


def _nsa_q_weights(w_q):
    q = w_q[:, :N_HEADS * HEAD_DIM]
    gates = w_q[:, N_HEADS * HEAD_DIM:].reshape(-1, N_HEADS, N_NSA_BRANCH)
    gates = jnp.transpose(gates, (0, 2, 1)).reshape(-1, N_NSA_BRANCH * N_HEADS)
    w = jnp.concatenate([q, _pad_cols(gates, LANE)], axis=1).astype(BF16)
    plan = [(0, 1024, True, Q_SCALE, 0, 0, "spread"),
            (1024, 128, False, 1.0, 1, 0, "sigmoid")]
    return w, plan, [(2048, BF16), (LANE, F32)]


def _kv_weights(w_kv):
    kvw = w_kv.reshape(-1, 2 * N_NSA_BRANCH, N_KV_HEADS * HEAD_DIM)
    k_cmp, v_cmp, k_slc, v_slc, k_win, v_win = [kvw[:, j] for j in range(2 * N_NSA_BRANCH)]
    w = jnp.concatenate([_pad_heads(k_slc, N_KV_HEADS), _pad_heads(k_win, N_KV_HEADS),
                         _pad_heads(v_slc, N_KV_HEADS), _pad_heads(v_win, N_KV_HEADS),
                         k_cmp, v_cmp], axis=1).astype(BF16)
    plan = [(0, 512, True, 1.0, 0, 0, "blockhot"),
            (512, 512, True, 1.0, 0, 512, None),
            (1024, 512, False, 1.0, 0, 1024, "ones"),
            (1536, 512, False, 1.0, 0, 1536, "ones"),
            (2048, 256, True, 1.0, 1, 0, None),
            (2304, 256, False, 1.0, 1, 256, None)]
    return w, plan, [(2048, BF16), (512, F32)]


def _nsa_constants(s_len):
    n_cmp = (s_len - CMP_LEN) // CMP_STRIDE + 1
    n_slc = s_len // SLC_LEN
    cmp_start = np.arange(n_cmp) * CMP_STRIDE
    slc_start = np.arange(n_slc) * SLC_LEN
    ov = (np.minimum(cmp_start[:, None] + CMP_LEN, slc_start[None, :] + SLC_LEN)
          - np.maximum(cmp_start[:, None], slc_start[None, :]))
    agg = (np.clip(ov, 0, None) / CMP_LEN).astype(np.float32)
    agg_t = np.zeros((n_slc, s_len // CMP_STRIDE), np.float32)
    agg_t[:, :n_cmp] = agg.T
    return jnp.asarray(agg_t)


def kernel(x, c, positions, attn_gain, ffn_gain, w_ada, b_ada, a_w_in, a_w_out, b_w_q, b_w_out, kv_gain, w_kv_ada, b_kv_ada, w_kv, cmp_pe_k, cmp_w1_k, cmp_w2_k, cmp_pe_v, cmp_w1_v, cmp_w2_v, ffn_w_gate, ffn_w_up, ffn_w_down, moe_w_router, moe_w_gate, moe_w_up, moe_w_down, final_gain):
    b, s_len, d = x.shape
    assert s_len % TK == 0 and s_len // SLC_LEN <= HEAD_DIM and s_len >= WIN_SPAN and b <= 8

    cos_t, sin_t = _rope_tables(positions)
    c_pad = jnp.pad(c, ((0, 8 - b), (0, 0)))
    mod = _ada(c_pad, w_ada, b_ada)[:, :b]
    kv_mod = _ada(c_pad, w_kv_ada[None], b_kv_ada[None])[0, :b]

    def parts(m, n):
        return [p[:, None, :] for p in jnp.split(m, n, axis=-1)]

    a_shift, a_scale, a_gate, f_shift, f_scale, f_gate = parts(mod[0], 6)
    w, plan, outs = _dsa_weights(a_w_in[0])
    proj, iw = _norm_proj(x, attn_gain[0], a_shift, a_scale, cos_t, sin_t, w, plan, outs)
    bias = _dsa_bias(proj, iw, s_len)
    o = _dsa_attention(proj, bias)
    h = _outproj_residual(o, _out_rows(a_w_out[0]).astype(BF16), a_gate, x)
    h = _ffn_residual(h, ffn_gain[0], f_shift, f_scale, f_gate,
                      ffn_w_gate[0].astype(BF16), ffn_w_up[0].astype(BF16), ffn_w_down[0].astype(BF16))

    kv_shift, kv_scale = parts(kv_mod, 2)
    w, plan, outs = _kv_weights(w_kv)
    kv, kv_cmp = _norm_proj(h, kv_gain, kv_shift, kv_scale, cos_t, sin_t, w, plan, outs)
    nr = s_len // CMP_STRIDE
    xc = kv_cmp.reshape(b, s_len, 2, N_KV_HEADS, HEAD_DIM)
    xc = jnp.transpose(xc, (2, 0, 3, 1, 4)).reshape(2, b, N_KV_HEADS, nr, CMP_STRIDE * HEAD_DIM)
    pe = jnp.stack([cmp_pe_k, cmp_pe_v]).reshape(2, 2, 1, CMP_STRIDE * HEAD_DIM)
    w1 = jnp.stack([cmp_w1_k, cmp_w1_v]).reshape(2, 2, CMP_STRIDE * HEAD_DIM, CMP_HIDDEN).astype(BF16)
    w2 = jnp.stack([cmp_w2_k, cmp_w2_v]).astype(BF16)
    cmp = _compress(xc, pe, w1, w2)
    cmp = jnp.transpose(cmp, (0, 1, 3, 2, 4))
    cmp = jnp.pad(cmp, ((0, 0), (0, 0), (0, 0), (0, 0), (0, LANE - HEAD_DIM)))
    kc, vc = cmp.reshape(2, b, nr, N_KV_HEADS * LANE)

    a_shift, a_scale, a_gate, f_shift, f_scale, f_gate = parts(mod[1], 6)
    w, plan, outs = _nsa_q_weights(b_w_q[0])
    q, gates = _norm_proj(h, attn_gain[1], a_shift, a_scale, cos_t, sin_t, w, plan, outs)
    o = _nsa_attention(q, gates, kc, vc, kv, _nsa_constants(s_len))
    h = _outproj_residual(o, _out_rows(b_w_out[0]).astype(BF16), a_gate, h)
    return _moe_residual_norm(h, ffn_gain[1], f_shift, f_scale, f_gate, _pad_cols(moe_w_router[0], LANE),
                              final_gain, moe_w_gate[0].astype(BF16), moe_w_up[0].astype(BF16),
                              moe_w_down[0].astype(BF16))
```

```python
import functools

import numpy as np
import jax
import jax.numpy as jnp
from jax import lax
from jax.experimental import pallas as pl
from jax.experimental.pallas import tpu as pltpu

F32 = jnp.float32
BF16 = jnp.bfloat16

D_MODEL = 1024
HEAD_DIM = 64
N_HEADS = 16
N_KV_HEADS = 4
GROUP = N_HEADS // N_KV_HEADS
ROPE_THETA = 10000.0
RMS_EPS = 1e-6
NEG_INF = -1e30
MASKED = float("-inf")
POS_INF = 1e30
IDX_HEADS = 8
IDX_DIM = HEAD_DIM
DSA_TOPK = 256
CMP_LEN = 32
CMP_STRIDE = 16
CMP_HIDDEN = 256
SLC_LEN = 64
SLC_SHIFT = 6
SLC_TOPN = 16
WINDOW = 512
N_NSA_BRANCH = 3
N_EXPERTS = 8
TOP_K_EXPERTS = 2

LANE = 128
TQ = 256
TQI = 256
RB = 64
TK = 512
WIN_SPAN = WINDOW + TQ
TMX = 512
RBLK = 256
GRP = 512
INT_MIN = np.int32(-2 ** 31)
M_INIT = -3.0e38
Q_SCALE = HEAD_DIM ** -0.5 * float(np.log2(np.e))
VMEM_LIMIT = 56 * 1024 * 1024

_NT = (((1,), (1,)), ((), ()))


def _cparams(*sem, flags=None):
    return pltpu.CompilerParams(dimension_semantics=sem, vmem_limit_bytes=VMEM_LIMIT, flags=flags)


def _rope_kernel(pos_ref, inv_ref, sgn_ref, cos_ref, sin_ref):
    ang = pos_ref[0].astype(F32) * inv_ref[...]
    cos_ref[0] = jnp.cos(ang)
    sin_ref[0] = jnp.sin(ang) * sgn_ref[...]


def _rope_tables(positions):
    b, s = positions.shape
    inv = 1.0 / (ROPE_THETA ** (jnp.arange(0, HEAD_DIM, 2, dtype=F32) / HEAD_DIM))
    inv = jnp.tile(inv, LANE // (HEAD_DIM // 2))[None]
    sgn = np.where((np.arange(LANE) % HEAD_DIM) < HEAD_DIM // 2, -1.0, 1.0).astype(np.float32)[None]
    return pl.pallas_call(
        _rope_kernel,
        grid=(b,),
        in_specs=[pl.BlockSpec((1, s, 1), lambda i: (i, 0, 0)),
                  pl.BlockSpec((1, LANE), lambda i: (0, 0)),
                  pl.BlockSpec((1, LANE), lambda i: (0, 0))],
        out_specs=[pl.BlockSpec((1, s, LANE), lambda i: (i, 0, 0))] * 2,
        out_shape=[jax.ShapeDtypeStruct((b, s, LANE), F32)] * 2,
        compiler_params=_cparams("arbitrary"),
        name="rope_tables",
    )(positions[..., None], inv, jnp.asarray(sgn))


def _ada_kernel(c_ref, w_ref, b_ref, o_ref):
    c = c_ref[...]
    ca = c * jax.nn.sigmoid(c)
    o_ref[0] = jnp.dot(ca, w_ref[0], preferred_element_type=F32,
                       precision=lax.Precision.HIGHEST) + b_ref[0]


def _ada(c_pad, w, bias):
    nl, d, n = w.shape
    tn = 1024
    return pl.pallas_call(
        _ada_kernel,
        grid=(nl, n // tn),
        in_specs=[pl.BlockSpec((8, d), lambda l, j: (0, 0)),
                  pl.BlockSpec((1, d, tn), lambda l, j: (l, 0, j)),
                  pl.BlockSpec((1, 1, tn), lambda l, j: (l, 0, j))],
        out_specs=pl.BlockSpec((1, 8, tn), lambda l, j: (l, 0, j)),
        out_shape=jax.ShapeDtypeStruct((nl, 8, n), F32),
        compiler_params=_cparams("arbitrary", "arbitrary"),
        name="ada_mod",
    )(c_pad, w, bias.reshape(nl, 1, n))


def _norm_mod(x, g, shift, scale):
    ms = jnp.mean(x * x, axis=-1, keepdims=True)
    u = (x * lax.rsqrt(ms + RMS_EPS)) * g
    return u * (1.0 + scale) + shift


def _proj_kernel(h_ref, g_ref, sh_ref, sc_ref, cos_ref, sin_ref, w_ref, *out_refs, plan):
    ub = _norm_mod(h_ref[0], g_ref[...], sh_ref[0], sc_ref[0]).astype(BF16)
    cos = cos_ref[0]
    sin = sin_ref[0]
    lane = lax.broadcasted_iota(jnp.int32, cos.shape, 1)
    first = (lane & (HEAD_DIM - 1)) < HEAD_DIM // 2
    for (ws, width, rope, scale, oi, os_, act) in plan:
        for c0 in range(0, width, 256):
            cw = min(256, width - c0)
            y = jnp.dot(ub, w_ref[:, ws + c0: ws + c0 + cw], preferred_element_type=F32)
            for j in range(cw // LANE):
                blk = y[:, j * LANE:(j + 1) * LANE]
                if rope:
                    sw = jnp.where(first, pltpu.roll(blk, LANE - HEAD_DIM // 2, 1),
                                   pltpu.roll(blk, HEAD_DIM // 2, 1))
                    blk = blk * cos + sw * sin
                if scale != 1.0:
                    blk = blk * scale
                if act == "sigmoid":
                    blk = jax.nn.sigmoid(blk)
                kinds = () if act is None else act.split("+")
                dt = out_refs[oi].dtype

                def finish(x):
                    if "ones" in kinds:
                        x = jnp.where(lane == HEAD_DIM, 1.0, x)
                    if "blockhot" in kinds:
                        tok = pl.program_id(1) * cos.shape[0] + lax.broadcasted_iota(jnp.int32, cos.shape, 0)
                        x = jnp.where(lane - HEAD_DIM == (tok >> SLC_SHIFT), 1.0, x)
                    return x.astype(dt)

                if "spread" in kinds:
                    o0 = os_ + 2 * (c0 + j * LANE)
                    low = lane < HEAD_DIM
                    out_refs[oi][0, :, o0:o0 + LANE] = finish(jnp.where(low, blk, 0.0))
                    out_refs[oi][0, :, o0 + LANE:o0 + 2 * LANE] = finish(
                        jnp.where(low, pltpu.roll(blk, HEAD_DIM, 1), 0.0))
                else:
                    o0 = os_ + c0 + j * LANE
                    out_refs[oi][0, :, o0:o0 + LANE] = finish(blk)


def _norm_proj(h, gain, shift, scale, cos_t, sin_t, w, plan, outs, tm=512):
    b, s, d = h.shape
    n = w.shape[1]
    vec = pl.BlockSpec((1, 1, d), lambda bi, i: (bi, 0, 0))
    return pl.pallas_call(
        functools.partial(_proj_kernel, plan=tuple(plan)),
        grid=(b, s // tm),
        in_specs=[pl.BlockSpec((1, tm, d), lambda bi, i: (bi, i, 0)),
                  pl.BlockSpec((1, d), lambda bi, i: (0, 0)),
                  vec, vec,
                  pl.BlockSpec((1, tm, LANE), lambda bi, i: (bi, i, 0)),
                  pl.BlockSpec((1, tm, LANE), lambda bi, i: (bi, i, 0)),
                  pl.BlockSpec((d, n), lambda bi, i: (0, 0))],
        out_specs=[pl.BlockSpec((1, tm, wd), lambda bi, i: (bi, i, 0)) for wd, _ in outs],
        out_shape=[jax.ShapeDtypeStruct((b, s, wd), dt) for wd, dt in outs],
        compiler_params=_cparams("arbitrary", "arbitrary"),
        name="norm_proj",
    )(h, gain.reshape(1, d), shift, scale, cos_t, sin_t, w)


def _ordered_float(u):
    key = u ^ INT_MIN
    return lax.bitcast_convert_type(key ^ ((key >> 31) & np.int32(0x7FFFFFFF)), F32)


def _indexer_kernel(iq_ref, ik_ref, iw_ref, tri_ref, bias_ref, sc_scr, iqs_scr, iwb_scr, *, n_keep):
    i = pl.program_id(1)
    q0 = i * TQI
    s_len = bias_ref.shape[2]
    n_all = s_len // TK
    nch = (q0 + TQI + TK - 1) // TK
    t = q0 + lax.broadcasted_iota(jnp.int32, (TQI, 1), 0)
    iw = iw_ref[0]
    for hh in range(IDX_HEADS):
        iqs_scr[hh * TQI:(hh + 1) * TQI, :] = iq_ref[0, :, hh * LANE:(hh + 1) * LANE]
        iwb_scr[hh] = jnp.broadcast_to(iw[:, hh:hh + 1], (TQI, LANE))

    def lanes_of(c, j):
        return pl.ds(pl.multiple_of(c * TK + j * LANE, LANE), LANE)

    def score_chunk(c, carry):
        r = lax.dot_general(iqs_scr[...], ik_ref[0, pl.ds(pl.multiple_of(c * TK, TK), TK), :], _NT,
                            preferred_element_type=F32)
        pos = c * TK + lax.broadcasted_iota(jnp.int32, (1, LANE), 1)
        for j in range(TK // LANE):
            acc = jnp.zeros((TQI, LANE), F32)
            for hh in range(IDX_HEADS):
                acc = acc + iwb_scr[hh] * jnp.maximum(r[hh * TQI:(hh + 1) * TQI, j * LANE:(j + 1) * LANE], 0.0)
            sc_scr[:, lanes_of(c, j)] = jnp.where(pos + j * LANE <= t, acc, MASKED)
        return carry

    lax.fori_loop(0, nch, score_chunk, 0)

    def lane_sum(x):
        return jnp.sum(x, axis=1, keepdims=True)

    assert n_all * (TK // LANE) < 256
    b8, b16 = np.int32(1 << 8), np.int32(1 << 16)
    blocks = list(range(0, TQI, RB))
    thr_u = [jnp.zeros((RB, 1), jnp.int32) for _ in blocks]
    cnt_thr = [jnp.zeros((RB, 1), F32) + (nch * TK).astype(F32) for _ in blocks]
    for bit in range(30, -1, -2):
        cus = [[thr_u[bi] | np.array([m << bit], np.uint32).view(np.int32)[0] for m in (1, 2, 3)]
               for bi in range(len(blocks))]
        cbs = [[jnp.broadcast_to(_ordered_float(u), (RB, LANE)) for u in cu] for cu in cus]
        parts = []
        for bi, rb in enumerate(blocks):
            def body(c, part, cb=cbs[bi], rb=rb):
                for j in range(TK // LANE):
                    x = sc_scr[rb:rb + RB, lanes_of(c, j)]
                    part = part + jnp.where(x >= cb[2], 1 + b8 + b16,
                                            jnp.where(x >= cb[1], 1 + b8, jnp.where(x >= cb[0], 1, 0)))
                return part

            parts.append(lax.fori_loop(0, nch, body, jnp.zeros((RB, LANE), jnp.int32)))
        for bi in range(len(blocks)):
            for m, sh in enumerate((0, 8, 16)):
                cnt = lane_sum(((parts[bi] >> sh) & 0xFF).astype(F32))
                ok = cnt >= n_keep
                thr_u[bi] = jnp.where(ok, cus[bi][m], thr_u[bi])
                cnt_thr[bi] = jnp.where(ok, cnt, cnt_thr[bi])

    thr, has_thr, tied = [], [], None
    for bi in range(len(blocks)):
        f = _ordered_float(thr_u[bi])
        has = f > MASKED
        thr.append(jnp.where(has, f, MASKED))
        has_thr.append(has)
        any_tie = jnp.max(jnp.where(has & (cnt_thr[bi] > n_keep), 1, 0)) > 0
        tied = any_tie if tied is None else jnp.logical_or(tied, any_tie)

    @pl.when(jnp.logical_not(tied))
    def _():
        for bi, rb in enumerate(blocks):
            thr_b = jnp.broadcast_to(thr[bi], (RB, LANE))

            def body(c, carry, thr_b=thr_b, rb=rb):
                for j in range(TK // LANE):
                    x = sc_scr[rb:rb + RB, lanes_of(c, j)]
                    bias_ref[0, rb:rb + RB, lanes_of(c, j)] = (
                        jnp.where((x >= thr_b) & (x > MASKED), 0.0, NEG_INF).astype(BF16))
                return carry
            lax.fori_loop(0, nch, body, 0)

    @pl.when(tied)
    def _():
        for bi, rb in enumerate(blocks):
            th, has = thr[bi], has_thr[bi]

            def count_gt(c, part, th=th, rb=rb):
                x = sc_scr[rb:rb + RB, pl.ds(pl.multiple_of(c * TK, TK), TK)]
                return part + lane_sum(jnp.where(x > th, 1.0, 0.0))
            need = n_keep - lax.fori_loop(0, nch, count_gt, jnp.zeros((RB, 1), F32))

            def body(c, seen, th=th, has=has, need=need, rb=rb):
                x = sc_scr[rb:rb + RB, pl.ds(pl.multiple_of(c * TK, TK), TK)]
                tie = x == th
                pref = jnp.dot(jnp.where(tie, 1.0, 0.0).astype(BF16), tri_ref[...],
                               preferred_element_type=F32) + seen
                sel = ((x > th) | (tie & (pref <= need) & has)) & (x > MASKED)
                bias_ref[0, rb:rb + RB, pl.ds(pl.multiple_of(c * TK, TK), TK)] = (
                    jnp.where(sel, 0.0, NEG_INF).astype(BF16))
                return pref[:, TK - 1:TK]
            lax.fori_loop(0, nch, body, jnp.zeros((RB, 1), F32))

    def fill(c, carry):
        bias_ref[0, :, pl.ds(pl.multiple_of(c * TK, TK), TK)] = jnp.full((TQI, TK), NEG_INF, BF16)
        return carry
    lax.fori_loop(nch, n_all, fill, 0)


def _dsa_bias(proj, iw, s_len):
    b = proj.shape[0]
    n_keep = min(DSA_TOPK, s_len // 4)
    tri = jnp.asarray(np.triu(np.ones((TK, TK), np.float32)), BF16)
    return pl.pallas_call(
        functools.partial(_indexer_kernel, n_keep=n_keep),
        grid=(b, s_len // TQI),
        in_specs=[pl.BlockSpec((1, TQI, IDX_HEADS * LANE), lambda bi, i: (bi, i, 2)),
                  pl.BlockSpec((1, s_len, LANE), lambda bi, i: (bi, 0, 32)),
                  pl.BlockSpec((1, TQI, LANE), lambda bi, i: (bi, i, 0)),
                  pl.BlockSpec((TK, TK), lambda bi, i: (0, 0))],
        out_specs=pl.BlockSpec((1, TQI, s_len), lambda bi, i: (bi, i, 0)),
        out_shape=jax.ShapeDtypeStruct((b, s_len, s_len), BF16),
        scratch_shapes=[pltpu.VMEM((TQI, s_len), F32),
                        pltpu.VMEM((IDX_HEADS * TQI, LANE), BF16),
                        pltpu.VMEM((IDX_HEADS, TQI, LANE), F32)],
        compiler_params=_cparams("arbitrary", "arbitrary"),
        name="dsa_indexer",
    )(proj, proj, iw, tri)


def _stack_heads(q_ref, qs_scr):
    for g in range(N_KV_HEADS):
        for r in range(GROUP):
            hh = g * GROUP + r
            qs_scr[g, r * TQ:(r + 1) * TQ, :] = q_ref[0, :, hh * LANE:(hh + 1) * LANE]


def _online_update(s, v, m_scr, acc_scr, g):
    m_old = m_scr[g]
    m_new = jnp.maximum(m_old, jnp.max(s, axis=1, keepdims=True))
    alpha = jnp.exp2(m_old - m_new)
    p = jnp.exp2(s - jnp.concatenate([m_new] * (s.shape[1] // LANE), axis=1))
    acc_scr[g] = alpha * acc_scr[g] + jnp.dot(p.astype(BF16), v, preferred_element_type=F32)
    m_scr[g] = m_new


def _init_online(m_scr, acc_scr):
    m_scr[...] = jnp.full(m_scr.shape, M_INIT, F32)
    acc_scr[...] = jnp.zeros(acc_scr.shape, F32)


def _normalised(acc):
    return acc / acc[:, HEAD_DIM:HEAD_DIM + 1]


def _head(g):
    return slice(g * LANE, (g + 1) * LANE)


def _out_rows(w_out):
    d = w_out.shape[1]
    w = w_out.reshape(N_HEADS, HEAD_DIM, d)
    return jnp.pad(w, ((0, 0), (0, LANE - HEAD_DIM), (0, 0))).reshape(N_HEADS * LANE, d)


def _dsa_attn_kernel(q_ref, k_ref, v_ref, bias_ref, o_ref, qs_scr, m_scr, acc_scr):
    i = pl.program_id(1)
    nch = (i * TQ + TQ + TK - 1) // TK
    _stack_heads(q_ref, qs_scr)
    _init_online(m_scr, acc_scr)

    def chunk(c, carry):
        k0 = pl.multiple_of(c * TK, TK)
        bias = bias_ref[0, :, pl.ds(k0, TK)].astype(F32)
        bias4 = jnp.concatenate([bias] * GROUP, axis=0)
        for g in range(N_KV_HEADS):
            s = lax.dot_general(qs_scr[g], k_ref[0, pl.ds(k0, TK), _head(g)], _NT,
                                preferred_element_type=F32) + bias4
            _online_update(s, v_ref[0, pl.ds(k0, TK), _head(g)], m_scr, acc_scr, g)
        return carry

    lax.fori_loop(0, nch, chunk, 0)
    for g in range(N_KV_HEADS):
        og = _normalised(acc_scr[g])
        for r in range(GROUP):
            hh = g * GROUP + r
            o_ref[0, :, hh * LANE:(hh + 1) * LANE] = og[r * TQ:(r + 1) * TQ].astype(BF16)


def _dsa_attention(proj, bias):
    b, s_len, _ = proj.shape
    return pl.pallas_call(
        _dsa_attn_kernel,
        grid=(b, s_len // TQ),
        in_specs=[pl.BlockSpec((1, TQ, N_HEADS * LANE), lambda bi, i: (bi, i, 0)),
                  pl.BlockSpec((1, s_len, N_KV_HEADS * LANE), lambda bi, i: (bi, 0, 6)),
                  pl.BlockSpec((1, s_len, N_KV_HEADS * LANE), lambda bi, i: (bi, 0, 7)),
                  pl.BlockSpec((1, TQ, s_len), lambda bi, i: (bi, i, 0))],
        out_specs=pl.BlockSpec((1, TQ, N_HEADS * LANE), lambda bi, i: (bi, i, 0)),
        out_shape=jax.ShapeDtypeStruct((b, s_len, N_HEADS * LANE), BF16),
        scratch_shapes=[pltpu.VMEM((N_KV_HEADS, GROUP * TQ, LANE), BF16),
                        pltpu.VMEM((N_KV_HEADS, GROUP * TQ, LANE), F32),
                        pltpu.VMEM((N_KV_HEADS, GROUP * TQ, LANE), F32)],
        compiler_params=_cparams("arbitrary", "arbitrary"),
        name="dsa_attention",
    )(proj, proj, proj, bias)


def _outproj_kernel(o_ref, w_ref, gate_ref, h_ref, out_ref):
    y = jnp.dot(o_ref[0], w_ref[...], preferred_element_type=F32)
    out_ref[0] = h_ref[0] + gate_ref[0] * y


def _outproj_residual(o, w, gate, h, tm=512):
    b, s, d = h.shape
    k = o.shape[2]
    return pl.pallas_call(
        _outproj_kernel,
        grid=(b, s // tm),
        in_specs=[pl.BlockSpec((1, tm, k), lambda bi, i: (bi, i, 0)),
                  pl.BlockSpec((k, d), lambda bi, i: (0, 0)),
                  pl.BlockSpec((1, 1, d), lambda bi, i: (bi, 0, 0)),
                  pl.BlockSpec((1, tm, d), lambda bi, i: (bi, i, 0))],
        out_specs=pl.BlockSpec((1, tm, d), lambda bi, i: (bi, i, 0)),
        out_shape=jax.ShapeDtypeStruct((b, s, d), F32),
        compiler_params=_cparams("arbitrary", "arbitrary"),
        name="outproj_residual",
    )(o, w, gate, h)


def _ffn_kernel(h_ref, g_ref, sh_ref, sc_ref, gate_ref, wg_ref, wu_ref, wd_ref, out_ref, u_scr, acc_scr):
    f = pl.program_id(2)

    @pl.when(f == 0)
    def _():
        u_scr[...] = _norm_mod(h_ref[0], g_ref[...], sh_ref[0], sc_ref[0]).astype(BF16)
        acc_scr[...] = jnp.zeros(acc_scr.shape, F32)

    ub = u_scr[...]
    a = jnp.dot(ub, wg_ref[...], preferred_element_type=F32)
    up = jnp.dot(ub, wu_ref[...], preferred_element_type=F32)
    hid = (a * jax.nn.sigmoid(a)) * up
    acc_scr[...] += jnp.dot(hid.astype(BF16), wd_ref[...], preferred_element_type=F32)

    @pl.when(f == pl.num_programs(2) - 1)
    def _():
        out_ref[0] = h_ref[0] + gate_ref[0] * acc_scr[...]


def _ffn_residual(h, gain, shift, scale, gate, wg, wu, wd, tm=512):
    b, s, d = h.shape
    dff = wg.shape[1]
    tf = dff
    vec = pl.BlockSpec((1, 1, d), lambda bi, i, f: (bi, 0, 0))
    once = pl.Buffered(1)
    return pl.pallas_call(
        _ffn_kernel,
        grid=(b, s // tm, dff // tf),
        in_specs=[pl.BlockSpec((1, tm, d), lambda bi, i, f: (bi, i, 0)),
                  pl.BlockSpec((1, d), lambda bi, i, f: (0, 0)),
                  vec, vec, vec,
                  pl.BlockSpec((d, tf), lambda bi, i, f: (0, f), pipeline_mode=once),
                  pl.BlockSpec((d, tf), lambda bi, i, f: (0, f), pipeline_mode=once),
                  pl.BlockSpec((tf, d), lambda bi, i, f: (f, 0), pipeline_mode=once)],
        out_specs=pl.BlockSpec((1, tm, d), lambda bi, i, f: (bi, i, 0)),
        out_shape=jax.ShapeDtypeStruct((b, s, d), F32),
        scratch_shapes=[pltpu.VMEM((tm, d), BF16), pltpu.VMEM((tm, d), F32)],
        compiler_params=_cparams("arbitrary", "arbitrary", "arbitrary"),
        name="ffn_residual",
    )(h, gain.reshape(1, d), shift, scale, gate, wg, wu, wd)


def _compress_kernel(x_ref, pe_ref, w1_ref, w2_ref, o_ref):
    x = x_ref[0, 0, 0]
    nr = x.shape[0]
    pa = jnp.dot((x + pe_ref[0, 0]).astype(BF16), w1_ref[0, 0], preferred_element_type=F32)
    pb = jnp.dot((x + pe_ref[0, 1]).astype(BF16), w1_ref[0, 1], preferred_element_type=F32)
    hid = pa + pltpu.roll(pb, nr - 1, 0)
    act = 0.5 * hid * (1.0 + jnp.tanh(np.float32(np.sqrt(2.0 / np.pi)) * (hid + 0.044715 * (hid * hid * hid))))
    o_ref[0, 0, 0] = jnp.dot(act.astype(BF16), w2_ref[0], preferred_element_type=F32).astype(BF16)


def _compress(x, pe, w1, w2):
    two, b, g, nr, kk = x.shape
    return pl.pallas_call(
        _compress_kernel,
        grid=(two, b, g),
        in_specs=[pl.BlockSpec((1, 1, 1, nr, kk), lambda a, bi, gi: (a, bi, gi, 0, 0)),
                  pl.BlockSpec((1, 2, 1, kk), lambda a, bi, gi: (a, 0, 0, 0)),
                  pl.BlockSpec((1, 2, kk, CMP_HIDDEN), lambda a, bi, gi: (a, 0, 0, 0)),
                  pl.BlockSpec((1, CMP_HIDDEN, HEAD_DIM), lambda a, bi, gi: (a, 0, 0))],
        out_specs=pl.BlockSpec((1, 1, 1, nr, HEAD_DIM), lambda a, bi, gi: (a, bi, gi, 0, 0)),
        out_shape=jax.ShapeDtypeStruct((two, b, g, nr, HEAD_DIM), BF16),
        compiler_params=_cparams("arbitrary", "arbitrary", "arbitrary"),
        name="nsa_compress",
    )(x, pe, w1, w2)


def _nsa_kernel(q_ref, gt_ref, kc_ref, vc_ref, ks_ref, kw_ref, vs_ref, vw_ref, aggt_ref, o_ref,
                qs_scr, oc_scr, m_scr, acc_scr, *, n_sel):
    i = pl.program_id(1)
    q0 = i * TQ
    nch = (q0 + TQ + TK - 1) // TK
    _stack_heads(q_ref, qs_scr)
    _init_online(m_scr, acc_scr)
    t = q0 + lax.broadcasted_iota(jnp.int32, (TQ, 1), 0)
    t4 = jnp.concatenate([t] * GROUP, axis=0)

    n_cmp = kc_ref.shape[1]
    cmp_end = lax.broadcasted_iota(jnp.int32, (1, n_cmp), 1) * CMP_STRIDE + (CMP_LEN - 1)
    cmask = cmp_end <= t4
    n_blk = aggt_ref.shape[0]
    jrow = lax.broadcasted_iota(jnp.int32, (n_blk, TQ), 0)
    tq = q0 + lax.broadcasted_iota(jnp.int32, (n_blk, TQ), 1)
    jt = tq >> SLC_SHIFT
    valid = jrow * SLC_LEN <= tq
    forced = (jrow == 0) | (jrow == jt) | (jrow == jt - 1)
    for g in range(N_KV_HEADS):
        sc = lax.dot_general(qs_scr[g], kc_ref[0, :, g * LANE:(g + 1) * LANE], _NT, preferred_element_type=F32)
        sm = jnp.where(cmask, sc, NEG_INF)
        e = jnp.where(cmask, jnp.exp2(sm - jnp.max(sm, axis=1, keepdims=True)), 0.0)
        pc = e / jnp.maximum(jnp.sum(e, axis=1, keepdims=True), 1e-30)
        oc_scr[g] = jnp.dot(pc.astype(BF16), vc_ref[0, :, _head(g)], preferred_element_type=F32)
        pcs = pc[0:TQ]
        for r in range(1, GROUP):
            pcs = pcs + pc[r * TQ:(r + 1) * TQ]
        imp = lax.dot_general(aggt_ref[...], pcs, _NT, preferred_element_type=F32,
                              precision=lax.Precision.HIGHEST)
        imp = jnp.where(valid, jnp.where(forced, POS_INF, imp), NEG_INF)
        rank = jnp.zeros((n_blk, TQ), jnp.int32)
        for j in range(n_blk):
            row = imp[j:j + 1, :]
            beats = (row > imp) | ((row == imp) & (jrow > j))
            rank = rank + beats.astype(jnp.int32)
        sel = (rank < n_sel) & valid
        selb = jnp.where(sel, 0.0, NEG_INF)
        pads = [jnp.zeros((HEAD_DIM, TQ), F32), selb]
        if n_blk < LANE - HEAD_DIM:
            pads.append(jnp.zeros((LANE - HEAD_DIM - n_blk, TQ), F32))
        selb = jnp.concatenate(pads, axis=0)
        selb = selb.T.astype(BF16)
        qs_scr[g] = qs_scr[g] + jnp.concatenate([selb] * GROUP, axis=0)

    def slc_chunk(c, diagonal):
        k0 = pl.multiple_of(c * TK, TK)
        for g in range(N_KV_HEADS):
            s = lax.dot_general(qs_scr[g], ks_ref[0, pl.ds(k0, TK), _head(g)], _NT,
                                preferred_element_type=F32)
            if diagonal:
                pos = k0 + lax.broadcasted_iota(jnp.int32, (1, TK), 1)
                s = jnp.where(pos <= t4, s, NEG_INF)
            _online_update(s, vs_ref[0, pl.ds(k0, TK), _head(g)], m_scr, acc_scr, g)

    def slc_body(c, carry):
        slc_chunk(c, False)
        return carry

    lax.fori_loop(0, nch - 1, slc_body, 0)
    slc_chunk(nch - 1, True)

    gt = gt_ref[0]
    w0 = pl.multiple_of(jnp.maximum(q0 - WINDOW, 0), TQ)
    wpos = w0 + lax.broadcasted_iota(jnp.int32, (1, WIN_SPAN), 1)
    wmask = (wpos <= t4) & (wpos > t4 - WINDOW)
    for g in range(N_KV_HEADS):
        s = lax.dot_general(qs_scr[g], kw_ref[0, pl.ds(w0, WIN_SPAN), _head(g)], _NT,
                            preferred_element_type=F32)
        s = jnp.where(wmask, s, NEG_INF)
        p = jnp.exp2(s - jnp.max(s, axis=1, keepdims=True))
        p = jnp.where(wmask, p, 0.0)
        ow = _normalised(jnp.dot(p.astype(BF16), vw_ref[0, pl.ds(w0, WIN_SPAN), _head(g)],
                                 preferred_element_type=F32))
        osl = _normalised(acc_scr[g])
        oc = oc_scr[g]
        for r in range(GROUP):
            hh = g * GROUP + r
            rs = slice(r * TQ, (r + 1) * TQ)
            o = (gt[:, hh:hh + 1] * oc[rs] + gt[:, N_HEADS + hh:N_HEADS + hh + 1] * osl[rs]
                 + gt[:, 2 * N_HEADS + hh:2 * N_HEADS + hh + 1] * ow[rs])
            o_ref[0, :, hh * LANE:(hh + 1) * LANE] = o.astype(BF16)


def _nsa_attention(q, gates, kc, vc, kv, agg_t):
    b, s_len, _ = q.shape
    n_cmp = kc.shape[1]
    n_blk = agg_t.shape[0]
    n_sel = min(SLC_TOPN, s_len // SLC_LEN)
    rows = GROUP * TQ
    return pl.pallas_call(
        functools.partial(_nsa_kernel, n_sel=n_sel),
        grid=(b, s_len // TQ),
        in_specs=[pl.BlockSpec((1, TQ, N_HEADS * LANE), lambda bi, i: (bi, i, 0)),
                  pl.BlockSpec((1, TQ, LANE), lambda bi, i: (bi, i, 0)),
                  pl.BlockSpec((1, n_cmp, N_KV_HEADS * LANE), lambda bi, i: (bi, 0, 0)),
                  pl.BlockSpec((1, n_cmp, N_KV_HEADS * LANE), lambda bi, i: (bi, 0, 0)),
                  pl.BlockSpec((1, s_len, N_KV_HEADS * LANE), lambda bi, i: (bi, 0, 0), pipeline_mode=pl.Buffered(1)),
                  pl.BlockSpec((1, s_len, N_KV_HEADS * LANE), lambda bi, i: (bi, 0, 1), pipeline_mode=pl.Buffered(1)),
                  pl.BlockSpec((1, s_len, N_KV_HEADS * LANE), lambda bi, i: (bi, 0, 2), pipeline_mode=pl.Buffered(1)),
                  pl.BlockSpec((1, s_len, N_KV_HEADS * LANE), lambda bi, i: (bi, 0, 3), pipeline_mode=pl.Buffered(1)),
                  pl.BlockSpec((n_blk, n_cmp), lambda bi, i: (0, 0))],
        out_specs=pl.BlockSpec((1, TQ, N_HEADS * LANE), lambda bi, i: (bi, i, 0)),
        out_shape=jax.ShapeDtypeStruct((b, s_len, N_HEADS * LANE), BF16),
        scratch_shapes=[pltpu.VMEM((N_KV_HEADS, rows, LANE), BF16),
                        pltpu.VMEM((N_KV_HEADS, rows, LANE), F32),
                        pltpu.VMEM((N_KV_HEADS, rows, LANE), F32),
                        pltpu.VMEM((N_KV_HEADS, rows, LANE), F32)],
        compiler_params=_cparams("arbitrary", "arbitrary"),
        name="nsa_attention",
    )(q, gates, kc, vc, kv, kv, kv, kv, agg_t)


def _router_kernel(h_ref, g_ref, sh_ref, sc_ref, wr_ref, low_ref, u_ref, info_ref, cnt_ref, carry_scr):
    @pl.when((pl.program_id(0) == 0) & (pl.program_id(1) == 0))
    def _():
        carry_scr[...] = jnp.zeros(carry_scr.shape, F32)

    u = _norm_mod(h_ref[0], g_ref[...], sh_ref[0], sc_ref[0])
    u_ref[0] = u.astype(BF16)
    logits = jnp.dot(u, wr_ref[...], preferred_element_type=F32, precision=lax.Precision.HIGHEST)
    lane = lax.broadcasted_iota(jnp.int32, logits.shape, 1)
    logits = jnp.where(lane < N_EXPERTS, logits, -jnp.inf)
    m1 = jnp.max(logits, axis=1, keepdims=True)
    i1 = jnp.min(jnp.where(logits == m1, lane, LANE), axis=1, keepdims=True)
    rest = jnp.where(lane == i1, -jnp.inf, logits)
    m2 = jnp.max(rest, axis=1, keepdims=True)
    i2 = jnp.min(jnp.where(rest == m2, lane, LANE), axis=1, keepdims=True)
    e2 = jnp.exp(m2 - m1)
    w1 = 1.0 / (1.0 + e2)
    w2 = e2 / (1.0 + e2)
    hit1 = lane == i1
    hit2 = lane == i2
    onehot = jnp.where(hit1 | hit2, 1.0, 0.0).astype(BF16)
    ahead = jnp.dot(low_ref[...], onehot, preferred_element_type=F32) + carry_scr[0:1, :]
    pos1 = jnp.sum(jnp.where(hit1, ahead, 0.0), axis=1, keepdims=True)
    pos2 = jnp.sum(jnp.where(hit2, ahead, 0.0), axis=1, keepdims=True)
    tile_cnt = jnp.dot(jnp.ones((8, TMX), BF16), onehot, preferred_element_type=F32)
    cnt_ref[0] = tile_cnt
    carry_scr[...] += tile_cnt
    info = jnp.zeros(logits.shape, F32)
    for j, v in enumerate((i1.astype(F32), i2.astype(F32), pos1, pos2, w1, w2)):
        info = jnp.where(lane == j, v, info)
    info_ref[0] = info


def _moe_gather_kernel(gb_ref, gs_ref, u_ref, dr_ref, w_ref, xs_ref, ws_ref):
    d = pl.program_id(0)
    xs_ref[...] = jnp.zeros(xs_ref.shape, BF16)
    ws_ref[...] = jnp.zeros(ws_ref.shape, F32)
    row = d * RBLK + lax.broadcasted_iota(jnp.int32, (RBLK, TMX), 0)

    def pair(k, carry):
        s = gs_ref[k]
        p1 = dr_ref[s, 0:1, :] == row
        p2 = dr_ref[s, 1:2, :] == row
        onehot = jnp.where(p1 | p2, 1.0, 0.0).astype(BF16)
        tok = u_ref[pl.ds(pl.multiple_of(s * TMX, TMX), TMX), :]
        xs_ref[...] += jnp.dot(onehot, tok, preferred_element_type=F32).astype(BF16)
        wrow = jnp.where(p1, w_ref[s, 0:1, :], 0.0) + jnp.where(p2, w_ref[s, 1:2, :], 0.0)
        ws_ref[...] += jnp.sum(wrow, axis=1, keepdims=True)
        return carry

    lax.fori_loop(gb_ref[d], gb_ref[d + 1], pair, 0)


def _moe_ffn_kernel(te_ref, nt_ref, x_ref, ws_ref, wg_ref, wu_ref, wd_ref, o_ref, acc_scr):
    i = pl.program_id(0)
    f = pl.program_id(1)
    live = i < nt_ref[0]

    @pl.when(f == 0)
    def _():
        acc_scr[...] = jnp.zeros(acc_scr.shape, F32)

    @pl.when(live)
    def _():
        x = x_ref[...]
        a = jnp.dot(x, wg_ref[0], preferred_element_type=F32)
        up = jnp.dot(x, wu_ref[0], preferred_element_type=F32)
        hid = (a * jax.nn.sigmoid(a)) * up
        acc_scr[...] += jnp.dot(hid.astype(BF16), wd_ref[0], preferred_element_type=F32)

    @pl.when(f == pl.num_programs(1) - 1)
    def _():
        w = jnp.concatenate([ws_ref[...]] * (acc_scr.shape[1] // LANE), axis=1)
        o_ref[...] = jnp.where(live, acc_scr[...] * w, 0.0).astype(BF16)


def _moe_combine_kernel(ys_ref, yd_ref, nv_ref, o_ref, dr_ref, h_ref, gate_ref, fg_ref, out_ref, y_scr):
    k = pl.program_id(0)
    last_k = pl.num_programs(0) - 1
    s = ys_ref[k]
    d = yd_ref[k]

    @pl.when((k == 0) | (ys_ref[jnp.maximum(k - 1, 0)] != s))
    def _():
        y_scr[...] = jnp.zeros(y_scr.shape, F32)

    @pl.when(k < nv_ref[0])
    def _():
        col = d * RBLK + lax.broadcasted_iota(jnp.int32, (TMX, RBLK), 1)
        dr = dr_ref[...]
        onehot = jnp.where((dr[:, 0:1] == col) | (dr[:, 1:2] == col), 1.0, 0.0).astype(BF16)
        y_scr[...] += jnp.dot(onehot, o_ref[...], preferred_element_type=F32)

    @pl.when((k == last_k) | (ys_ref[jnp.minimum(k + 1, last_k)] != s))
    def _():
        hn = h_ref[0] + gate_ref[0] * y_scr[...]
        ms = jnp.mean(hn * hn, axis=-1, keepdims=True)
        out_ref[0] = (hn * lax.rsqrt(ms + RMS_EPS)) * fg_ref[...]


def _moe_plan(info, cnt, n_tok):
    nt = n_tok // TMX
    n_blk = (TOP_K_EXPERTS * n_tok + N_EXPERTS * GRP) // RBLK
    n_pairs = n_blk + N_EXPERTS * nt
    e1, e2, pos1, pos2 = [info[:, j].astype(jnp.int32) for j in range(4)]
    counts = cnt[:, 0, :N_EXPERTS].astype(jnp.int32)
    padded = (jnp.sum(counts, axis=0) + GRP - 1) // GRP * GRP
    ends = jnp.cumsum(padded)
    start = ends - padded
    dr1 = start[e1] + pos1
    dr2 = start[e2] + pos2
    first = start[None, :] + jnp.cumsum(counts, axis=0) - counts
    last = first + counts - 1
    fb = first // RBLK
    nb = jnp.where(counts > 0, last // RBLK - fb + 1, 0)
    k3 = jnp.arange(3)
    cd = (fb[..., None] + k3).reshape(-1)
    cs = jnp.broadcast_to(jnp.arange(nt)[:, None, None], (nt, N_EXPERTS, 3)).reshape(-1)
    cv = (k3 < nb[..., None]).reshape(-1)
    n_valid = jnp.sum(cv.astype(jnp.int32))

    def ordered(key):
        smaller = cv[None, :] & (key[None, :] < key[:, None])
        slot = jnp.where(cv, jnp.sum(smaller.astype(jnp.int32), axis=1), -1)
        want = jnp.minimum(jnp.arange(n_pairs), n_valid - 1)
        pick = slot[None, :] == want[:, None]
        return (jnp.sum(jnp.where(pick, cd[None, :], 0), axis=1).astype(jnp.int32),
                jnp.sum(jnp.where(pick, cs[None, :], 0), axis=1).astype(jnp.int32))

    gd, gs = ordered(cd * nt + cs)
    blocks = jnp.arange(n_blk + 1)
    gb = jnp.sum((cv[None, :] & (cd[None, :] < blocks[:, None])).astype(jnp.int32), axis=1)
    yd, ys = ordered(cs * n_blk + cd)
    n_ffn = n_blk * RBLK // GRP
    tile_row = jnp.arange(n_ffn) * GRP
    tile_e = jnp.minimum(jnp.sum((ends[None, :] <= tile_row[:, None]).astype(jnp.int32), axis=1), N_EXPERTS - 1)
    return dict(dr1=dr1, dr2=dr2, gb=gb.astype(jnp.int32), gs=gs, yd=yd, ys=ys, n_valid=n_valid.reshape(1),
                tile_e=tile_e.astype(jnp.int32), n_live=(ends[-1] // GRP).astype(jnp.int32).reshape(1),
                n_blk=n_blk, n_pairs=n_pairs, n_ffn=n_ffn)


def _moe_residual_norm(h, gain, shift, scale, gate, w_router, final_gain, wg, wu, wd, tf=1792):
    b, s, d = h.shape
    n_tok = b * s
    tpb = s // TMX
    nt = n_tok // TMX
    ne, _, dff = wg.shape
    vec = pl.BlockSpec((1, 1, d), lambda bi, i: (bi, 0, 0))
    low = jnp.asarray(np.tril(np.ones((TMX, TMX), np.float32), -1), BF16)
    u, info, cnt = pl.pallas_call(
        _router_kernel,
        grid=(b, tpb),
        in_specs=[pl.BlockSpec((1, TMX, d), lambda bi, i: (bi, i, 0)),
                  pl.BlockSpec((1, d), lambda bi, i: (0, 0)),
                  vec, vec,
                  pl.BlockSpec((d, LANE), lambda bi, i: (0, 0)),
                  pl.BlockSpec((TMX, TMX), lambda bi, i: (0, 0))],
        out_specs=[pl.BlockSpec((1, TMX, d), lambda bi, i: (bi, i, 0)),
                   pl.BlockSpec((1, TMX, LANE), lambda bi, i: (bi, i, 0)),
                   pl.BlockSpec((1, 8, LANE), lambda bi, i: (bi * tpb + i, 0, 0))],
        out_shape=[jax.ShapeDtypeStruct((b, s, d), BF16),
                   jax.ShapeDtypeStruct((b, s, LANE), F32),
                   jax.ShapeDtypeStruct((nt, 8, LANE), F32)],
        scratch_shapes=[pltpu.VMEM((8, LANE), F32)],
        compiler_params=_cparams("arbitrary", "arbitrary"),
        name="moe_router",
    )(h, gain.reshape(1, d), shift, scale, w_router, low)

    info = info.reshape(n_tok, LANE)
    plan = _moe_plan(info, cnt, n_tok)
    n_blk, n_pairs, n_ffn = plan["n_blk"], plan["n_pairs"], plan["n_ffn"]
    pad6 = jnp.zeros((nt, 6, TMX), jnp.int32)
    dr_rows = jnp.concatenate([plan["dr1"].reshape(nt, 1, TMX), plan["dr2"].reshape(nt, 1, TMX), pad6], axis=1)
    w_rows = jnp.concatenate([info[:, 4].reshape(nt, 1, TMX), info[:, 5].reshape(nt, 1, TMX),
                              pad6.astype(F32)], axis=1)
    dr_cols = jnp.pad(jnp.stack([plan["dr1"], plan["dr2"]], axis=1), ((0, 0), (0, LANE - 2)))

    xs, ws = pl.pallas_call(
        _moe_gather_kernel,
        grid_spec=pltpu.PrefetchScalarGridSpec(
            num_scalar_prefetch=2,
            grid=(n_blk,),
            in_specs=[pl.BlockSpec((n_tok, d), lambda k, gb, gs: (0, 0), pipeline_mode=pl.Buffered(1)),
                      pl.BlockSpec((nt, 8, TMX), lambda k, gb, gs: (0, 0, 0), pipeline_mode=pl.Buffered(1)),
                      pl.BlockSpec((nt, 8, TMX), lambda k, gb, gs: (0, 0, 0), pipeline_mode=pl.Buffered(1))],
            out_specs=[pl.BlockSpec((RBLK, d), lambda k, gb, gs: (k, 0)),
                       pl.BlockSpec((RBLK, LANE), lambda k, gb, gs: (k, 0))]),
        out_shape=[jax.ShapeDtypeStruct((n_blk * RBLK, d), BF16),
                   jax.ShapeDtypeStruct((n_blk * RBLK, LANE), F32)],
        compiler_params=_cparams("arbitrary"),
        name="moe_gather",
    )(plan["gb"], plan["gs"], u.reshape(n_tok, d), dr_rows, w_rows)

    def live_tile(i, te, nl):
        return jnp.maximum(jnp.minimum(i, nl[0] - 1), 0)

    rows_out = pl.pallas_call(
        _moe_ffn_kernel,
        grid_spec=pltpu.PrefetchScalarGridSpec(
            num_scalar_prefetch=2,
            grid=(n_ffn, dff // tf),
            in_specs=[pl.BlockSpec((GRP, d), lambda i, f, te, nl: (live_tile(i, te, nl), 0)),
                      pl.BlockSpec((GRP, LANE), lambda i, f, te, nl: (live_tile(i, te, nl), 0)),
                      pl.BlockSpec((1, d, tf), lambda i, f, te, nl: (te[i], 0, f)),
                      pl.BlockSpec((1, d, tf), lambda i, f, te, nl: (te[i], 0, f)),
                      pl.BlockSpec((1, tf, d), lambda i, f, te, nl: (te[i], f, 0))],
            out_specs=pl.BlockSpec((GRP, d), lambda i, f, te, nl: (i, 0)),
            scratch_shapes=[pltpu.VMEM((GRP, d), F32)]),
        out_shape=jax.ShapeDtypeStruct((n_blk * RBLK, d), BF16),
        compiler_params=_cparams("arbitrary", "arbitrary"),
        name="moe_experts",
    )(plan["tile_e"], plan["n_live"], xs, ws, wg, wu, wd)

    return pl.pallas_call(
        _moe_combine_kernel,
        grid_spec=pltpu.PrefetchScalarGridSpec(
            num_scalar_prefetch=3,
            grid=(n_pairs,),
            in_specs=[pl.BlockSpec((RBLK, d), lambda k, ys, yd, nv: (yd[k], 0)),
                      pl.BlockSpec((TMX, LANE), lambda k, ys, yd, nv: (ys[k], 0)),
                      pl.BlockSpec((1, TMX, d), lambda k, ys, yd, nv: (ys[k] // tpb, ys[k] % tpb, 0)),
                      pl.BlockSpec((1, 1, d), lambda k, ys, yd, nv: (ys[k] // tpb, 0, 0)),
                      pl.BlockSpec((1, d), lambda k, ys, yd, nv: (0, 0))],
            out_specs=pl.BlockSpec((1, TMX, d), lambda k, ys, yd, nv: (ys[k] // tpb, ys[k] % tpb, 0)),
            scratch_shapes=[pltpu.VMEM((TMX, d), F32)]),
        out_shape=jax.ShapeDtypeStruct((b, s, d), F32),
        compiler_params=_cparams("arbitrary"),
        name="moe_combine_norm",
    )(plan["ys"], plan["yd"], plan["n_valid"], rows_out, dr_cols, h, gate, final_gain.reshape(1, d))


def _pad_cols(w, n):
    return jnp.pad(w, ((0, 0), (0, n - w.shape[1])))


def _dsa_weights(w_in):
    q, k, v, iq, ik, iw = jnp.split(w_in, [1024, 1280, 1536, 2048, 2112], axis=1)
    w = jnp.concatenate([q, iq, k, v, _pad_cols(ik, LANE), _pad_cols(iw, LANE)], axis=1).astype(BF16)
    plan = [(0, 1024, True, Q_SCALE, 0, 0, "spread"),
            (1024, 512, True, 1.0, 0, 2048, "spread"),
            (1536, 256, True, 1.0, 0, 3072, "spread"),
            (1792, 256, False, 1.0, 0, 3584, "spread+ones"),
            (2048, 128, True, 1.0, 0, 4096, None),
            (2176, 128, False, IDX_HEADS ** -0.5 * IDX_DIM ** -0.5, 1, 0, None)]
    return w, plan, [(4224, BF16), (LANE, F32)]


def _nsa_q_weights(w_q):
    q = w_q[:, :N_HEADS * HEAD_DIM]
    gates = w_q[:, N_HEADS * HEAD_DIM:].reshape(-1, N_HEADS, N_NSA_BRANCH)
    gates = jnp.transpose(gates, (0, 2, 1)).reshape(-1, N_NSA_BRANCH * N_HEADS)
    w = jnp.concatenate([q, _pad_cols(gates, LANE)], axis=1).astype(BF16)
    plan = [(0, 1024, True, Q_SCALE, 0, 0, "spread"),
            (1024, 128, False, 1.0, 1, 0, "sigmoid")]
    return w, plan, [(2048, BF16), (LANE, F32)]


def _kv_weights(w_kv):
    kvw = w_kv.reshape(-1, 2 * N_NSA_BRANCH, N_KV_HEADS * HEAD_DIM)
    k_cmp, v_cmp, k_slc, v_slc, k_win, v_win = [kvw[:, j] for j in range(2 * N_NSA_BRANCH)]
    w = jnp.concatenate([k_slc, k_win, v_slc, v_win, k_cmp, v_cmp], axis=1).astype(BF16)
    plan = [(0, 256, True, 1.0, 0, 0, "spread+blockhot"),
            (256, 256, True, 1.0, 0, 512, "spread"),
            (512, 256, False, 1.0, 0, 1024, "spread+ones"),
            (768, 256, False, 1.0, 0, 1536, "spread+ones"),
            (1024, 256, True, 1.0, 1, 0, None),
            (1280, 256, False, 1.0, 1, 256, None)]
    return w, plan, [(2048, BF16), (512, F32)]


def _nsa_constants(s_len):
    n_cmp = (s_len - CMP_LEN) // CMP_STRIDE + 1
    n_slc = s_len // SLC_LEN
    cmp_start = np.arange(n_cmp) * CMP_STRIDE
    slc_start = np.arange(n_slc) * SLC_LEN
    ov = (np.minimum(cmp_start[:, None] + CMP_LEN, slc_start[None, :] + SLC_LEN)
          - np.maximum(cmp_start[:, None], slc_start[None, :]))
    agg = (np.clip(ov, 0, None) / CMP_LEN).astype(np.float32)
    agg_t = np.zeros((n_slc, s_len // CMP_STRIDE), np.float32)
    agg_t[:, :n_cmp] = agg.T
    return jnp.asarray(agg_t)


def kernel(x, c, positions, attn_gain, ffn_gain, w_ada, b_ada, a_w_in, a_w_out, b_w_q, b_w_out, kv_gain, w_kv_ada, b_kv_ada, w_kv, cmp_pe_k, cmp_w1_k, cmp_w2_k, cmp_pe_v, cmp_w1_v, cmp_w2_v, ffn_w_gate, ffn_w_up, ffn_w_down, moe_w_router, moe_w_gate, moe_w_up, moe_w_down, final_gain):
    b, s_len, d = x.shape
    assert s_len % TK == 0 and s_len // SLC_LEN <= HEAD_DIM and s_len >= WIN_SPAN and b <= 8

    cos_t, sin_t = _rope_tables(positions)
    c_pad = jnp.pad(c, ((0, 8 - b), (0, 0)))
    mod = _ada(c_pad, w_ada, b_ada)[:, :b]
    kv_mod = _ada(c_pad, w_kv_ada[None], b_kv_ada[None])[0, :b]

    def parts(m, n):
        return [p[:, None, :] for p in jnp.split(m, n, axis=-1)]

    a_shift, a_scale, a_gate, f_shift, f_scale, f_gate = parts(mod[0], 6)
    w, plan, outs = _dsa_weights(a_w_in[0])
    proj, iw = _norm_proj(x, attn_gain[0], a_shift, a_scale, cos_t, sin_t, w, plan, outs)
    bias = _dsa_bias(proj, iw, s_len)
    o = _dsa_attention(proj, bias)
    h = _outproj_residual(o, _out_rows(a_w_out[0]).astype(BF16), a_gate, x)
    h = _ffn_residual(h, ffn_gain[0], f_shift, f_scale, f_gate,
                      ffn_w_gate[0].astype(BF16), ffn_w_up[0].astype(BF16), ffn_w_down[0].astype(BF16))

    kv_shift, kv_scale = parts(kv_mod, 2)
    w, plan, outs = _kv_weights(w_kv)
    kv, kv_cmp = _norm_proj(h, kv_gain, kv_shift, kv_scale, cos_t, sin_t, w, plan, outs)
    nr = s_len // CMP_STRIDE
    xc = kv_cmp.reshape(b, s_len, 2, N_KV_HEADS, HEAD_DIM)
    xc = jnp.transpose(xc, (2, 0, 3, 1, 4)).reshape(2, b, N_KV_HEADS, nr, CMP_STRIDE * HEAD_DIM)
    pe = jnp.stack([cmp_pe_k, cmp_pe_v]).reshape(2, 2, 1, CMP_STRIDE * HEAD_DIM)
    w1 = jnp.stack([cmp_w1_k, cmp_w1_v]).reshape(2, 2, CMP_STRIDE * HEAD_DIM, CMP_HIDDEN).astype(BF16)
    w2 = jnp.stack([cmp_w2_k, cmp_w2_v]).astype(BF16)
    cmp = _compress(xc, pe, w1, w2)
    cmp = jnp.transpose(cmp, (0, 1, 3, 2, 4))
    cmp = jnp.pad(cmp, ((0, 0), (0, 0), (0, 0), (0, 0), (0, LANE - HEAD_DIM)))
    kc, vc = cmp.reshape(2, b, nr, N_KV_HEADS * LANE)

    a_shift, a_scale, a_gate, f_shift, f_scale, f_gate = parts(mod[1], 6)
    w, plan, outs = _nsa_q_weights(b_w_q[0])
    q, gates = _norm_proj(h, attn_gain[1], a_shift, a_scale, cos_t, sin_t, w, plan, outs)
    o = _nsa_attention(q, gates, kc, vc, kv, _nsa_constants(s_len))
    h = _outproj_residual(o, _out_rows(b_w_out[0]).astype(BF16), a_gate, h)
    return _moe_residual_norm(h, ffn_gain[1], f_shift, f_scale, f_gate, _pad_cols(moe_w_router[0], LANE),
                              final_gain, moe_w_gate[0].astype(BF16), moe_w_up[0].astype(BF16),
                              moe_w_down[0].astype(BF16))
```

```python
import functools

import numpy as np
import jax
import jax.numpy as jnp
from jax import lax
from jax.experimental import pallas as pl
from jax.experimental.pallas import tpu as pltpu

F32 = jnp.float32
BF16 = jnp.bfloat16

D_MODEL = 1024
HEAD_DIM = 64
N_HEADS = 16
N_KV_HEADS = 4
GROUP = N_HEADS // N_KV_HEADS
ROPE_THETA = 10000.0
RMS_EPS = 1e-6
NEG_INF = -1e30
MASKED = float("-inf")
POS_INF = 1e30
IDX_HEADS = 8
IDX_DIM = HEAD_DIM
DSA_TOPK = 256
CMP_LEN = 32
CMP_STRIDE = 16
CMP_HIDDEN = 256
SLC_LEN = 64
SLC_SHIFT = 6
SLC_TOPN = 16
WINDOW = 512
N_NSA_BRANCH = 3
N_EXPERTS = 8
TOP_K_EXPERTS = 2

LANE = 128
TQ = 256
TQI = 256
RB = 64
TK = 512
WIN_SPAN = WINDOW + TQ
TMX = 512
RBLK = 256
GRP = 512
INT_MIN = np.int32(-2 ** 31)
M_INIT = -3.0e38
Q_SCALE = HEAD_DIM ** -0.5 * float(np.log2(np.e))
VMEM_LIMIT = 56 * 1024 * 1024

_NT = (((1,), (1,)), ((), ()))


def _cparams(*sem, flags=None):
    return pltpu.CompilerParams(dimension_semantics=sem, vmem_limit_bytes=VMEM_LIMIT, flags=flags)


def _rope_kernel(pos_ref, inv_ref, sgn_ref, cos_ref, sin_ref):
    ang = pos_ref[0].astype(F32) * inv_ref[...]
    cos_ref[0] = jnp.cos(ang)
    sin_ref[0] = jnp.sin(ang) * sgn_ref[...]


def _rope_tables(positions):
    b, s = positions.shape
    inv = 1.0 / (ROPE_THETA ** (jnp.arange(0, HEAD_DIM, 2, dtype=F32) / HEAD_DIM))
    inv = jnp.tile(inv, LANE // (HEAD_DIM // 2))[None]
    sgn = np.where((np.arange(LANE) % HEAD_DIM) < HEAD_DIM // 2, -1.0, 1.0).astype(np.float32)[None]
    return pl.pallas_call(
        _rope_kernel,
        grid=(b,),
        in_specs=[pl.BlockSpec((1, s, 1), lambda i: (i, 0, 0)),
                  pl.BlockSpec((1, LANE), lambda i: (0, 0)),
                  pl.BlockSpec((1, LANE), lambda i: (0, 0))],
        out_specs=[pl.BlockSpec((1, s, LANE), lambda i: (i, 0, 0))] * 2,
        out_shape=[jax.ShapeDtypeStruct((b, s, LANE), F32)] * 2,
        compiler_params=_cparams("arbitrary"),
        name="rope_tables",
    )(positions[..., None], inv, jnp.asarray(sgn))


def _ada_kernel(c_ref, w_ref, b_ref, o_ref):
    c = c_ref[...]
    ca = c * jax.nn.sigmoid(c)
    o_ref[0] = jnp.dot(ca, w_ref[0], preferred_element_type=F32,
                       precision=lax.Precision.HIGHEST) + b_ref[0]


def _ada(c_pad, w, bias):
    nl, d, n = w.shape
    tn = 1024
    return pl.pallas_call(
        _ada_kernel,
        grid=(nl, n // tn),
        in_specs=[pl.BlockSpec((8, d), lambda l, j: (0, 0)),
                  pl.BlockSpec((1, d, tn), lambda l, j: (l, 0, j)),
                  pl.BlockSpec((1, 1, tn), lambda l, j: (l, 0, j))],
        out_specs=pl.BlockSpec((1, 8, tn), lambda l, j: (l, 0, j)),
        out_shape=jax.ShapeDtypeStruct((nl, 8, n), F32),
        compiler_params=_cparams("arbitrary", "arbitrary"),
        name="ada_mod",
    )(c_pad, w, bias.reshape(nl, 1, n))


def _norm_mod(x, g, shift, scale):
    ms = jnp.mean(x * x, axis=-1, keepdims=True)
    u = (x * lax.rsqrt(ms + RMS_EPS)) * g
    return u * (1.0 + scale) + shift


def _proj_kernel(h_ref, g_ref, sh_ref, sc_ref, cos_ref, sin_ref, w_ref, *out_refs, plan):
    ub = _norm_mod(h_ref[0], g_ref[...], sh_ref[0], sc_ref[0]).astype(BF16)
    cos = cos_ref[0]
    sin = sin_ref[0]
    lane = lax.broadcasted_iota(jnp.int32, cos.shape, 1)
    first = (lane & (HEAD_DIM - 1)) < HEAD_DIM // 2
    for (ws, width, rope, scale, oi, os_, act) in plan:
        for c0 in range(0, width, 256):
            cw = min(256, width - c0)
            y = jnp.dot(ub, w_ref[:, ws + c0: ws + c0 + cw], preferred_element_type=F32)
            for j in range(cw // LANE):
                blk = y[:, j * LANE:(j + 1) * LANE]
                if rope:
                    sw = jnp.where(first, pltpu.roll(blk, LANE - HEAD_DIM // 2, 1),
                                   pltpu.roll(blk, HEAD_DIM // 2, 1))
                    blk = blk * cos + sw * sin
                if scale != 1.0:
                    blk = blk * scale
                if act == "sigmoid":
                    blk = jax.nn.sigmoid(blk)
                kinds = () if act is None else act.split("+")
                dt = out_refs[oi].dtype

                def finish(x):
                    if "ones" in kinds:
                        x = jnp.where(lane == HEAD_DIM, 1.0, x)
                    if "blockhot" in kinds:
                        tok = pl.program_id(1) * cos.shape[0] + lax.broadcasted_iota(jnp.int32, cos.shape, 0)
                        x = jnp.where(lane - HEAD_DIM == (tok >> SLC_SHIFT), 1.0, x)
                    return x.astype(dt)

                if "spread" in kinds:
                    o0 = os_ + 2 * (c0 + j * LANE)
                    low = lane < HEAD_DIM
                    out_refs[oi][0, :, o0:o0 + LANE] = finish(jnp.where(low, blk, 0.0))
                    out_refs[oi][0, :, o0 + LANE:o0 + 2 * LANE] = finish(
                        jnp.where(low, pltpu.roll(blk, HEAD_DIM, 1), 0.0))
                else:
                    o0 = os_ + c0 + j * LANE
                    out_refs[oi][0, :, o0:o0 + LANE] = finish(blk)


def _norm_proj(h, gain, shift, scale, cos_t, sin_t, w, plan, outs, tm=512):
    b, s, d = h.shape
    n = w.shape[1]
    vec = pl.BlockSpec((1, 1, d), lambda bi, i: (bi, 0, 0))
    return pl.pallas_call(
        functools.partial(_proj_kernel, plan=tuple(plan)),
        grid=(b, s // tm),
        in_specs=[pl.BlockSpec((1, tm, d), lambda bi, i: (bi, i, 0)),
                  pl.BlockSpec((1, d), lambda bi, i: (0, 0)),
                  vec, vec,
                  pl.BlockSpec((1, tm, LANE), lambda bi, i: (bi, i, 0)),
                  pl.BlockSpec((1, tm, LANE), lambda bi, i: (bi, i, 0)),
                  pl.BlockSpec((d, n), lambda bi, i: (0, 0))],
        out_specs=[pl.BlockSpec((1, tm, wd), lambda bi, i: (bi, i, 0)) for wd, _ in outs],
        out_shape=[jax.ShapeDtypeStruct((b, s, wd), dt) for wd, dt in outs],
        compiler_params=_cparams("arbitrary", "arbitrary"),
        name="norm_proj",
    )(h, gain.reshape(1, d), shift, scale, cos_t, sin_t, w)


def _ordered_float(u):
    key = u ^ INT_MIN
    return lax.bitcast_convert_type(key ^ ((key >> 31) & np.int32(0x7FFFFFFF)), F32)


def _indexer_kernel(iq_ref, ik_ref, iw_ref, tri_ref, bias_ref, sc_scr, iqs_scr, iwb_scr, *, n_keep):
    i = pl.program_id(1)
    q0 = i * TQI
    s_len = bias_ref.shape[2]
    n_all = s_len // TK
    nch = (q0 + TQI + TK - 1) // TK
    t = q0 + lax.broadcasted_iota(jnp.int32, (TQI, 1), 0)
    iw = iw_ref[0]
    for hh in range(IDX_HEADS):
        iqs_scr[hh * TQI:(hh + 1) * TQI, :] = iq_ref[0, :, hh * LANE:(hh + 1) * LANE]
        iwb_scr[hh] = jnp.broadcast_to(iw[:, hh:hh + 1], (TQI, LANE))

    def lanes_of(c, j):
        return pl.ds(pl.multiple_of(c * TK + j * LANE, LANE), LANE)

    def score_chunk(c, carry):
        r = lax.dot_general(iqs_scr[...], ik_ref[0, pl.ds(pl.multiple_of(c * TK, TK), TK), :], _NT,
                            preferred_element_type=F32)
        pos = c * TK + lax.broadcasted_iota(jnp.int32, (1, LANE), 1)
        for j in range(TK // LANE):
            acc = jnp.zeros((TQI, LANE), F32)
            for hh in range(IDX_HEADS):
                acc = acc + iwb_scr[hh] * jnp.maximum(r[hh * TQI:(hh + 1) * TQI, j * LANE:(j + 1) * LANE], 0.0)
            sc_scr[:, lanes_of(c, j)] = jnp.where(pos + j * LANE <= t, acc, MASKED)
        return carry

    lax.fori_loop(0, nch, score_chunk, 0)

    def lane_sum(x):
        return jnp.sum(x, axis=1, keepdims=True)

    assert n_all * (TK // LANE) < 256
    b8, b16 = np.int32(1 << 8), np.int32(1 << 16)
    blocks = list(range(0, TQI, RB))
    thr_u = [jnp.zeros((RB, 1), jnp.int32) for _ in blocks]
    cnt_thr = [jnp.zeros((RB, 1), F32) + (nch * TK).astype(F32) for _ in blocks]
    for bit in range(30, -1, -2):
        cus = [[thr_u[bi] | np.array([m << bit], np.uint32).view(np.int32)[0] for m in (1, 2, 3)]
               for bi in range(len(blocks))]
        cbs = [[jnp.broadcast_to(_ordered_float(u), (RB, LANE)) for u in cu] for cu in cus]
        parts = []
        for bi, rb in enumerate(blocks):
            def body(c, part, cb=cbs[bi], rb=rb):
                for j in range(TK // LANE):
                    x = sc_scr[rb:rb + RB, lanes_of(c, j)]
                    part = part + jnp.where(x >= cb[2], 1 + b8 + b16,
                                            jnp.where(x >= cb[1], 1 + b8, jnp.where(x >= cb[0], 1, 0)))
                return part

            parts.append(lax.fori_loop(0, nch, body, jnp.zeros((RB, LANE), jnp.int32)))
        for bi in range(len(blocks)):
            for m, sh in enumerate((0, 8, 16)):
                cnt = lane_sum(((parts[bi] >> sh) & 0xFF).astype(F32))
                ok = cnt >= n_keep
                thr_u[bi] = jnp.where(ok, cus[bi][m], thr_u[bi])
                cnt_thr[bi] = jnp.where(ok, cnt, cnt_thr[bi])

    thr, has_thr, tied = [], [], None
    for bi in range(len(blocks)):
        f = _ordered_float(thr_u[bi])
        has = f > MASKED
        thr.append(jnp.where(has, f, MASKED))
        has_thr.append(has)
        any_tie = jnp.max(jnp.where(has & (cnt_thr[bi] > n_keep), 1, 0)) > 0
        tied = any_tie if tied is None else jnp.logical_or(tied, any_tie)

    @pl.when(jnp.logical_not(tied))
    def _():
        for bi, rb in enumerate(blocks):
            thr_b = jnp.broadcast_to(thr[bi], (RB, LANE))

            def body(c, carry, thr_b=thr_b, rb=rb):
                for j in range(TK // LANE):
                    x = sc_scr[rb:rb + RB, lanes_of(c, j)]
                    bias_ref[0, rb:rb + RB, lanes_of(c, j)] = (
                        jnp.where((x >= thr_b) & (x > MASKED), 0.0, NEG_INF).astype(BF16))
                return carry
            lax.fori_loop(0, nch, body, 0)

    @pl.when(tied)
    def _():
        for bi, rb in enumerate(blocks):
            th, has = thr[bi], has_thr[bi]

            def count_gt(c, part, th=th, rb=rb):
                x = sc_scr[rb:rb + RB, pl.ds(pl.multiple_of(c * TK, TK), TK)]
                return part + lane_sum(jnp.where(x > th, 1.0, 0.0))
            need = n_keep - lax.fori_loop(0, nch, count_gt, jnp.zeros((RB, 1), F32))

            def body(c, seen, th=th, has=has, need=need, rb=rb):
                x = sc_scr[rb:rb + RB, pl.ds(pl.multiple_of(c * TK, TK), TK)]
                tie = x == th
                pref = jnp.dot(jnp.where(tie, 1.0, 0.0).astype(BF16), tri_ref[...],
                               preferred_element_type=F32) + seen
                sel = ((x > th) | (tie & (pref <= need) & has)) & (x > MASKED)
                bias_ref[0, rb:rb + RB, pl.ds(pl.multiple_of(c * TK, TK), TK)] = (
                    jnp.where(sel, 0.0, NEG_INF).astype(BF16))
                return pref[:, TK - 1:TK]
            lax.fori_loop(0, nch, body, jnp.zeros((RB, 1), F32))

    def fill(c, carry):
        bias_ref[0, :, pl.ds(pl.multiple_of(c * TK, TK), TK)] = jnp.full((TQI, TK), NEG_INF, BF16)
        return carry
    lax.fori_loop(nch, n_all, fill, 0)


def _dsa_bias(proj, iw, s_len):
    b = proj.shape[0]
    n_keep = min(DSA_TOPK, s_len // 4)
    tri = jnp.asarray(np.triu(np.ones((TK, TK), np.float32)), BF16)
    return pl.pallas_call(
        functools.partial(_indexer_kernel, n_keep=n_keep),
        grid=(b, s_len // TQI),
        in_specs=[pl.BlockSpec((1, TQI, IDX_HEADS * LANE), lambda bi, i: (bi, i, 2)),
                  pl.BlockSpec((1, s_len, LANE), lambda bi, i: (bi, 0, 32)),
                  pl.BlockSpec((1, TQI, LANE), lambda bi, i: (bi, i, 0)),
                  pl.BlockSpec((TK, TK), lambda bi, i: (0, 0))],
        out_specs=pl.BlockSpec((1, TQI, s_len), lambda bi, i: (bi, i, 0)),
        out_shape=jax.ShapeDtypeStruct((b, s_len, s_len), BF16),
        scratch_shapes=[pltpu.VMEM((TQI, s_len), F32),
                        pltpu.VMEM((IDX_HEADS * TQI, LANE), BF16),
                        pltpu.VMEM((IDX_HEADS, TQI, LANE), F32)],
        compiler_params=_cparams("arbitrary", "arbitrary"),
        name="dsa_indexer",
    )(proj, proj, iw, tri)


def _stack_heads(q_ref, qs_scr):
    for g in range(N_KV_HEADS):
        for r in range(GROUP):
            hh = g * GROUP + r
            qs_scr[g, r * TQ:(r + 1) * TQ, :] = q_ref[0, :, hh * LANE:(hh + 1) * LANE]


def _online_update(s, v, m_scr, acc_scr, g):
    m_old = m_scr[g]
    m_new = jnp.maximum(m_old, jnp.max(s, axis=1, keepdims=True))
    alpha = jnp.exp2(m_old - m_new)
    p = jnp.exp2(s - jnp.concatenate([m_new] * (s.shape[1] // LANE), axis=1))
    acc_scr[g] = alpha * acc_scr[g] + jnp.dot(p.astype(BF16), v, preferred_element_type=F32)
    m_scr[g] = m_new


def _init_online(m_scr, acc_scr):
    m_scr[...] = jnp.full(m_scr.shape, M_INIT, F32)
    acc_scr[...] = jnp.zeros(acc_scr.shape, F32)


def _normalised(acc):
    return acc / acc[:, HEAD_DIM:HEAD_DIM + 1]


def _head(g):
    return slice(g * LANE, (g + 1) * LANE)


def _out_rows(w_out):
    d = w_out.shape[1]
    w = w_out.reshape(N_HEADS, HEAD_DIM, d)
    return jnp.pad(w, ((0, 0), (0, LANE - HEAD_DIM), (0, 0))).reshape(N_HEADS * LANE, d)


def _dsa_attn_kernel(q_ref, k_ref, v_ref, bias_ref, o_ref, qs_scr, m_scr, acc_scr):
    i = pl.program_id(1)
    nch = (i * TQ + TQ + TK - 1) // TK
    _stack_heads(q_ref, qs_scr)
    _init_online(m_scr, acc_scr)

    def chunk(c, carry):
        k0 = pl.multiple_of(c * TK, TK)
        bias = bias_ref[0, :, pl.ds(k0, TK)].astype(F32)
        bias4 = jnp.concatenate([bias] * GROUP, axis=0)
        for g in range(N_KV_HEADS):
            s = lax.dot_general(qs_scr[g], k_ref[0, pl.ds(k0, TK), _head(g)], _NT,
                                preferred_element_type=F32) + bias4
            _online_update(s, v_ref[0, pl.ds(k0, TK), _head(g)], m_scr, acc_scr, g)
        return carry

    lax.fori_loop(0, nch, chunk, 0)
    for g in range(N_KV_HEADS):
        og = _normalised(acc_scr[g])
        for r in range(GROUP):
            hh = g * GROUP + r
            o_ref[0, :, hh * LANE:(hh + 1) * LANE] = og[r * TQ:(r + 1) * TQ].astype(BF16)


def _dsa_attention(proj, bias):
    b, s_len, _ = proj.shape
    return pl.pallas_call(
        _dsa_attn_kernel,
        grid=(b, s_len // TQ),
        in_specs=[pl.BlockSpec((1, TQ, N_HEADS * LANE), lambda bi, i: (bi, i, 0)),
                  pl.BlockSpec((1, s_len, N_KV_HEADS * LANE), lambda bi, i: (bi, 0, 6)),
                  pl.BlockSpec((1, s_len, N_KV_HEADS * LANE), lambda bi, i: (bi, 0, 7)),
                  pl.BlockSpec((1, TQ, s_len), lambda bi, i: (bi, i, 0))],
        out_specs=pl.BlockSpec((1, TQ, N_HEADS * LANE), lambda bi, i: (bi, i, 0)),
        out_shape=jax.ShapeDtypeStruct((b, s_len, N_HEADS * LANE), BF16),
        scratch_shapes=[pltpu.VMEM((N_KV_HEADS, GROUP * TQ, LANE), BF16),
                        pltpu.VMEM((N_KV_HEADS, GROUP * TQ, LANE), F32),
                        pltpu.VMEM((N_KV_HEADS, GROUP * TQ, LANE), F32)],
        compiler_params=_cparams("arbitrary", "arbitrary"),
        name="dsa_attention",
    )(proj, proj, proj, bias)


def _outproj_kernel(o_ref, w_ref, gate_ref, h_ref, out_ref):
    y = jnp.dot(o_ref[0], w_ref[...], preferred_element_type=F32)
    out_ref[0] = h_ref[0] + gate_ref[0] * y


def _outproj_residual(o, w, gate, h, tm=512):
    b, s, d = h.shape
    k = o.shape[2]
    return pl.pallas_call(
        _outproj_kernel,
        grid=(b, s // tm),
        in_specs=[pl.BlockSpec((1, tm, k), lambda bi, i: (bi, i, 0)),
                  pl.BlockSpec((k, d), lambda bi, i: (0, 0)),
                  pl.BlockSpec((1, 1, d), lambda bi, i: (bi, 0, 0)),
                  pl.BlockSpec((1, tm, d), lambda bi, i: (bi, i, 0))],
        out_specs=pl.BlockSpec((1, tm, d), lambda bi, i: (bi, i, 0)),
        out_shape=jax.ShapeDtypeStruct((b, s, d), F32),
        compiler_params=_cparams("arbitrary", "arbitrary"),
        name="outproj_residual",
    )(o, w, gate, h)


def _ffn_kernel(h_ref, g_ref, sh_ref, sc_ref, gate_ref, wg_ref, wu_ref, wd_ref, out_ref, u_scr, acc_scr):
    f = pl.program_id(2)

    @pl.when(f == 0)
    def _():
        u_scr[...] = _norm_mod(h_ref[0], g_ref[...], sh_ref[0], sc_ref[0]).astype(BF16)
        acc_scr[...] = jnp.zeros(acc_scr.shape, F32)

    ub = u_scr[...]
    a = jnp.dot(ub, wg_ref[...], preferred_element_type=F32)
    up = jnp.dot(ub, wu_ref[...], preferred_element_type=F32)
    hid = (a * jax.nn.sigmoid(a)) * up
    acc_scr[...] += jnp.dot(hid.astype(BF16), wd_ref[...], preferred_element_type=F32)

    @pl.when(f == pl.num_programs(2) - 1)
    def _():
        out_ref[0] = h_ref[0] + gate_ref[0] * acc_scr[...]


def _ffn_residual(h, gain, shift, scale, gate, wg, wu, wd, tm=512):
    b, s, d = h.shape
    dff = wg.shape[1]
    tf = dff
    vec = pl.BlockSpec((1, 1, d), lambda bi, i, f: (bi, 0, 0))
    once = pl.Buffered(1)
    return pl.pallas_call(
        _ffn_kernel,
        grid=(b, s // tm, dff // tf),
        in_specs=[pl.BlockSpec((1, tm, d), lambda bi, i, f: (bi, i, 0)),
                  pl.BlockSpec((1, d), lambda bi, i, f: (0, 0)),
                  vec, vec, vec,
                  pl.BlockSpec((d, tf), lambda bi, i, f: (0, f), pipeline_mode=once),
                  pl.BlockSpec((d, tf), lambda bi, i, f: (0, f), pipeline_mode=once),
                  pl.BlockSpec((tf, d), lambda bi, i, f: (f, 0), pipeline_mode=once)],
        out_specs=pl.BlockSpec((1, tm, d), lambda bi, i, f: (bi, i, 0)),
        out_shape=jax.ShapeDtypeStruct((b, s, d), F32),
        scratch_shapes=[pltpu.VMEM((tm, d), BF16), pltpu.VMEM((tm, d), F32)],
        compiler_params=_cparams("arbitrary", "arbitrary", "arbitrary"),
        name="ffn_residual",
    )(h, gain.reshape(1, d), shift, scale, gate, wg, wu, wd)


def _compress_kernel(x_ref, pe_ref, w1_ref, w2_ref, o_ref):
    x = x_ref[0, 0, 0]
    nr = x.shape[0]
    pa = jnp.dot((x + pe_ref[0, 0]).astype(BF16), w1_ref[0, 0], preferred_element_type=F32)
    pb = jnp.dot((x + pe_ref[0, 1]).astype(BF16), w1_ref[0, 1], preferred_element_type=F32)
    hid = pa + pltpu.roll(pb, nr - 1, 0)
    act = 0.5 * hid * (1.0 + jnp.tanh(np.float32(np.sqrt(2.0 / np.pi)) * (hid + 0.044715 * (hid * hid * hid))))
    o_ref[0, 0, 0] = jnp.dot(act.astype(BF16), w2_ref[0], preferred_element_type=F32).astype(BF16)


def _compress(x, pe, w1, w2):
    two, b, g, nr, kk = x.shape
    return pl.pallas_call(
        _compress_kernel,
        grid=(two, b, g),
        in_specs=[pl.BlockSpec((1, 1, 1, nr, kk), lambda a, bi, gi: (a, bi, gi, 0, 0)),
                  pl.BlockSpec((1, 2, 1, kk), lambda a, bi, gi: (a, 0, 0, 0)),
                  pl.BlockSpec((1, 2, kk, CMP_HIDDEN), lambda a, bi, gi: (a, 0, 0, 0)),
                  pl.BlockSpec((1, CMP_HIDDEN, HEAD_DIM), lambda a, bi, gi: (a, 0, 0))],
        out_specs=pl.BlockSpec((1, 1, 1, nr, HEAD_DIM), lambda a, bi, gi: (a, bi, gi, 0, 0)),
        out_shape=jax.ShapeDtypeStruct((two, b, g, nr, HEAD_DIM), BF16),
        compiler_params=_cparams("arbitrary", "arbitrary", "arbitrary"),
        name="nsa_compress",
    )(x, pe, w1, w2)


def _nsa_kernel(q_ref, gt_ref, kc_ref, vc_ref, ks_ref, kw_ref, vs_ref, vw_ref, aggt_ref, o_ref,
                qs_scr, oc_scr, m_scr, acc_scr, *, n_sel):
    i = pl.program_id(1)
    q0 = i * TQ
    nch = (q0 + TQ + TK - 1) // TK
    _stack_heads(q_ref, qs_scr)
    _init_online(m_scr, acc_scr)
    t = q0 + lax.broadcasted_iota(jnp.int32, (TQ, 1), 0)
    t4 = jnp.concatenate([t] * GROUP, axis=0)

    n_cmp = kc_ref.shape[1]
    cmp_end = lax.broadcasted_iota(jnp.int32, (1, n_cmp), 1) * CMP_STRIDE + (CMP_LEN - 1)
    cmask = cmp_end <= t4
    n_blk = aggt_ref.shape[0]
    jrow = lax.broadcasted_iota(jnp.int32, (n_blk, TQ), 0)
    tq = q0 + lax.broadcasted_iota(jnp.int32, (n_blk, TQ), 1)
    jt = tq >> SLC_SHIFT
    valid = jrow * SLC_LEN <= tq
    forced = (jrow == 0) | (jrow == jt) | (jrow == jt - 1)
    for g in range(N_KV_HEADS):
        sc = lax.dot_general(qs_scr[g], kc_ref[0, :, g * LANE:(g + 1) * LANE], _NT, preferred_element_type=F32)
        sm = jnp.where(cmask, sc, NEG_INF)
        e = jnp.where(cmask, jnp.exp2(sm - jnp.max(sm, axis=1, keepdims=True)), 0.0)
        pc = e / jnp.maximum(jnp.sum(e, axis=1, keepdims=True), 1e-30)
        oc_scr[g] = jnp.dot(pc.astype(BF16), vc_ref[0, :, _head(g)], preferred_element_type=F32)
        pcs = pc[0:TQ]
        for r in range(1, GROUP):
            pcs = pcs + pc[r * TQ:(r + 1) * TQ]
        imp = lax.dot_general(aggt_ref[...], pcs, _NT, preferred_element_type=F32,
                              precision=lax.Precision.HIGHEST)
        imp = jnp.where(valid, jnp.where(forced, POS_INF, imp), NEG_INF)
        rank = jnp.zeros((n_blk, TQ), jnp.int32)
        for j in range(n_blk):
            row = imp[j:j + 1, :]
            beats = (row > imp) | ((row == imp) & (jrow > j))
            rank = rank + beats.astype(jnp.int32)
        sel = (rank < n_sel) & valid
        selb = jnp.where(sel, 0.0, NEG_INF)
        pads = [jnp.zeros((HEAD_DIM, TQ), F32), selb]
        if n_blk < LANE - HEAD_DIM:
            pads.append(jnp.zeros((LANE - HEAD_DIM - n_blk, TQ), F32))
        selb = jnp.concatenate(pads, axis=0)
        selb = selb.T.astype(BF16)
        qs_scr[g] = qs_scr[g] + jnp.concatenate([selb] * GROUP, axis=0)

    def slc_chunk(c, diagonal):
        k0 = pl.multiple_of(c * TK, TK)
        for g in range(N_KV_HEADS):
            s = lax.dot_general(qs_scr[g], ks_ref[0, pl.ds(k0, TK), _head(g)], _NT,
                                preferred_element_type=F32)
            if diagonal:
                pos = k0 + lax.broadcasted_iota(jnp.int32, (1, TK), 1)
                s = jnp.where(pos <= t4, s, NEG_INF)
            _online_update(s, vs_ref[0, pl.ds(k0, TK), _head(g)], m_scr, acc_scr, g)

    def slc_body(c, carry):
        slc_chunk(c, False)
        return carry

    lax.fori_loop(0, nch - 1, slc_body, 0)
    slc_chunk(nch - 1, True)

    gt = gt_ref[0]
    w0 = pl.multiple_of(jnp.maximum(q0 - WINDOW, 0), TQ)
    wpos = w0 + lax.broadcasted_iota(jnp.int32, (1, WIN_SPAN), 1)
    wmask = (wpos <= t4) & (wpos > t4 - WINDOW)
    for g in range(N_KV_HEADS):
        s = lax.dot_general(qs_scr[g], kw_ref[0, pl.ds(w0, WIN_SPAN), _head(g)], _NT,
                            preferred_element_type=F32)
        s = jnp.where(wmask, s, NEG_INF)
        p = jnp.exp2(s - jnp.max(s, axis=1, keepdims=True))
        ow = _normalised(jnp.dot(p.astype(BF16), vw_ref[0, pl.ds(w0, WIN_SPAN), _head(g)],
                                 preferred_element_type=F32))
        osl = _normalised(acc_scr[g])
        oc = oc_scr[g]
        for r in range(GROUP):
            hh = g * GROUP + r
            rs = slice(r * TQ, (r + 1) * TQ)
            o = (gt[:, hh:hh + 1] * oc[rs] + gt[:, N_HEADS + hh:N_HEADS + hh + 1] * osl[rs]
                 + gt[:, 2 * N_HEADS + hh:2 * N_HEADS + hh + 1] * ow[rs])
            o_ref[0, :, hh * LANE:(hh + 1) * LANE] = o.astype(BF16)


def _nsa_attention(q, gates, kc, vc, kv, agg_t):
    b, s_len, _ = q.shape
    n_cmp = kc.shape[1]
    n_blk = agg_t.shape[0]
    n_sel = min(SLC_TOPN, s_len // SLC_LEN)
    rows = GROUP * TQ
    return pl.pallas_call(
        functools.partial(_nsa_kernel, n_sel=n_sel),
        grid=(b, s_len // TQ),
        in_specs=[pl.BlockSpec((1, TQ, N_HEADS * LANE), lambda bi, i: (bi, i, 0)),
                  pl.BlockSpec((1, TQ, LANE), lambda bi, i: (bi, i, 0)),
                  pl.BlockSpec((1, n_cmp, N_KV_HEADS * LANE), lambda bi, i: (bi, 0, 0)),
                  pl.BlockSpec((1, n_cmp, N_KV_HEADS * LANE), lambda bi, i: (bi, 0, 0)),
                  pl.BlockSpec((1, s_len, N_KV_HEADS * LANE), lambda bi, i: (bi, 0, 0), pipeline_mode=pl.Buffered(1)),
                  pl.BlockSpec((1, s_len, N_KV_HEADS * LANE), lambda bi, i: (bi, 0, 1), pipeline_mode=pl.Buffered(1)),
                  pl.BlockSpec((1, s_len, N_KV_HEADS * LANE), lambda bi, i: (bi, 0, 2), pipeline_mode=pl.Buffered(1)),
                  pl.BlockSpec((1, s_len, N_KV_HEADS * LANE), lambda bi, i: (bi, 0, 3), pipeline_mode=pl.Buffered(1)),
                  pl.BlockSpec((n_blk, n_cmp), lambda bi, i: (0, 0))],
        out_specs=pl.BlockSpec((1, TQ, N_HEADS * LANE), lambda bi, i: (bi, i, 0)),
        out_shape=jax.ShapeDtypeStruct((b, s_len, N_HEADS * LANE), BF16),
        scratch_shapes=[pltpu.VMEM((N_KV_HEADS, rows, LANE), BF16),
                        pltpu.VMEM((N_KV_HEADS, rows, LANE), F32),
                        pltpu.VMEM((N_KV_HEADS, rows, LANE), F32),
                        pltpu.VMEM((N_KV_HEADS, rows, LANE), F32)],
        compiler_params=_cparams("arbitrary", "arbitrary"),
        name="nsa_attention",
    )(q, gates, kc, vc, kv, kv, kv, kv, agg_t)


def _router_kernel(h_ref, g_ref, sh_ref, sc_ref, wr_ref, low_ref, u_ref, info_ref, cnt_ref, carry_scr):
    @pl.when((pl.program_id(0) == 0) & (pl.program_id(1) == 0))
    def _():
        carry_scr[...] = jnp.zeros(carry_scr.shape, F32)

    u = _norm_mod(h_ref[0], g_ref[...], sh_ref[0], sc_ref[0])
    u_ref[0] = u.astype(BF16)
    logits = jnp.dot(u, wr_ref[...], preferred_element_type=F32, precision=lax.Precision.HIGHEST)
    lane = lax.broadcasted_iota(jnp.int32, logits.shape, 1)
    logits = jnp.where(lane < N_EXPERTS, logits, -jnp.inf)
    m1 = jnp.max(logits, axis=1, keepdims=True)
    i1 = jnp.min(jnp.where(logits == m1, lane, LANE), axis=1, keepdims=True)
    rest = jnp.where(lane == i1, -jnp.inf, logits)
    m2 = jnp.max(rest, axis=1, keepdims=True)
    i2 = jnp.min(jnp.where(rest == m2, lane, LANE), axis=1, keepdims=True)
    e2 = jnp.exp(m2 - m1)
    w1 = 1.0 / (1.0 + e2)
    w2 = e2 / (1.0 + e2)
    hit1 = lane == i1
    hit2 = lane == i2
    onehot = jnp.where(hit1 | hit2, 1.0, 0.0).astype(BF16)
    ahead = jnp.dot(low_ref[...], onehot, preferred_element_type=F32) + carry_scr[0:1, :]
    pos1 = jnp.sum(jnp.where(hit1, ahead, 0.0), axis=1, keepdims=True)
    pos2 = jnp.sum(jnp.where(hit2, ahead, 0.0), axis=1, keepdims=True)
    tile_cnt = jnp.dot(jnp.ones((8, TMX), BF16), onehot, preferred_element_type=F32)
    cnt_ref[0] = tile_cnt
    carry_scr[...] += tile_cnt
    info = jnp.zeros(logits.shape, F32)
    for j, v in enumerate((i1.astype(F32), i2.astype(F32), pos1, pos2, w1, w2)):
        info = jnp.where(lane == j, v, info)
    info_ref[0] = info


def _moe_gather_kernel(gb_ref, gs_ref, u_ref, dr_ref, w_ref, xs_ref, ws_ref):
    d = pl.program_id(0)
    xs_ref[...] = jnp.zeros(xs_ref.shape, BF16)
    ws_ref[...] = jnp.zeros(ws_ref.shape, F32)
    row = d * RBLK + lax.broadcasted_iota(jnp.int32, (RBLK, TMX), 0)

    def pair(k, carry):
        s = gs_ref[k]
        p1 = dr_ref[s, 0:1, :] == row
        p2 = dr_ref[s, 1:2, :] == row
        onehot = jnp.where(p1 | p2, 1.0, 0.0).astype(BF16)
        tok = u_ref[pl.ds(pl.multiple_of(s * TMX, TMX), TMX), :]
        xs_ref[...] += jnp.dot(onehot, tok, preferred_element_type=F32).astype(BF16)
        wrow = jnp.where(p1, w_ref[s, 0:1, :], 0.0) + jnp.where(p2, w_ref[s, 1:2, :], 0.0)
        ws_ref[...] += jnp.sum(wrow, axis=1, keepdims=True)
        return carry

    lax.fori_loop(gb_ref[d], gb_ref[d + 1], pair, 0)


def _moe_ffn_kernel(te_ref, nt_ref, x_ref, ws_ref, wg_ref, wu_ref, wd_ref, o_ref, acc_scr):
    i = pl.program_id(0)
    f = pl.program_id(1)
    live = i < nt_ref[0]

    @pl.when(f == 0)
    def _():
        acc_scr[...] = jnp.zeros(acc_scr.shape, F32)

    @pl.when(live)
    def _():
        x = x_ref[...]
        a = jnp.dot(x, wg_ref[0], preferred_element_type=F32)
        up = jnp.dot(x, wu_ref[0], preferred_element_type=F32)
        hid = (a * jax.nn.sigmoid(a)) * up
        acc_scr[...] += jnp.dot(hid.astype(BF16), wd_ref[0], preferred_element_type=F32)

    @pl.when(f == pl.num_programs(1) - 1)
    def _():
        w = jnp.concatenate([ws_ref[...]] * (acc_scr.shape[1] // LANE), axis=1)
        o_ref[...] = jnp.where(live, acc_scr[...] * w, 0.0).astype(BF16)


def _moe_combine_kernel(ys_ref, yd_ref, nv_ref, o_ref, dr_ref, h_ref, gate_ref, fg_ref, out_ref, y_scr):
    k = pl.program_id(0)
    last_k = pl.num_programs(0) - 1
    s = ys_ref[k]
    d = yd_ref[k]

    @pl.when((k == 0) | (ys_ref[jnp.maximum(k - 1, 0)] != s))
    def _():
        y_scr[...] = jnp.zeros(y_scr.shape, F32)

    @pl.when(k < nv_ref[0])
    def _():
        col = d * RBLK + lax.broadcasted_iota(jnp.int32, (TMX, RBLK), 1)
        dr = dr_ref[...]
        onehot = jnp.where((dr[:, 0:1] == col) | (dr[:, 1:2] == col), 1.0, 0.0).astype(BF16)
        y_scr[...] += jnp.dot(onehot, o_ref[...], preferred_element_type=F32)

    @pl.when((k == last_k) | (ys_ref[jnp.minimum(k + 1, last_k)] != s))
    def _():
        hn = h_ref[0] + gate_ref[0] * y_scr[...]
        ms = jnp.mean(hn * hn, axis=-1, keepdims=True)
        out_ref[0] = (hn * lax.rsqrt(ms + RMS_EPS)) * fg_ref[...]


def _moe_plan(info, cnt, n_tok):
    nt = n_tok // TMX
    n_blk = (TOP_K_EXPERTS * n_tok + N_EXPERTS * GRP) // RBLK
    n_pairs = n_blk + N_EXPERTS * nt
    e1, e2, pos1, pos2 = [info[:, j].astype(jnp.int32) for j in range(4)]
    counts = cnt[:, 0, :N_EXPERTS].astype(jnp.int32)
    padded = (jnp.sum(counts, axis=0) + GRP - 1) // GRP * GRP
    ends = jnp.cumsum(padded)
    start = ends - padded
    dr1 = start[e1] + pos1
    dr2 = start[e2] + pos2
    first = start[None, :] + jnp.cumsum(counts, axis=0) - counts
    last = first + counts - 1
    fb = first // RBLK
    nb = jnp.where(counts > 0, last // RBLK - fb + 1, 0)
    k3 = jnp.arange(3)
    cd = (fb[..., None] + k3).reshape(-1)
    cs = jnp.broadcast_to(jnp.arange(nt)[:, None, None], (nt, N_EXPERTS, 3)).reshape(-1)
    cv = (k3 < nb[..., None]).reshape(-1)
    n_valid = jnp.sum(cv.astype(jnp.int32))

    def ordered(key):
        smaller = cv[None, :] & (key[None, :] < key[:, None])
        slot = jnp.where(cv, jnp.sum(smaller.astype(jnp.int32), axis=1), -1)
        want = jnp.minimum(jnp.arange(n_pairs), n_valid - 1)
        pick = slot[None, :] == want[:, None]
        return (jnp.sum(jnp.where(pick, cd[None, :], 0), axis=1).astype(jnp.int32),
                jnp.sum(jnp.where(pick, cs[None, :], 0), axis=1).astype(jnp.int32))

    gd, gs = ordered(cd * nt + cs)
    blocks = jnp.arange(n_blk + 1)
    gb = jnp.sum((cv[None, :] & (cd[None, :] < blocks[:, None])).astype(jnp.int32), axis=1)
    yd, ys = ordered(cs * n_blk + cd)
    n_ffn = n_blk * RBLK // GRP
    tile_row = jnp.arange(n_ffn) * GRP
    tile_e = jnp.minimum(jnp.sum((ends[None, :] <= tile_row[:, None]).astype(jnp.int32), axis=1), N_EXPERTS - 1)
    return dict(dr1=dr1, dr2=dr2, gb=gb.astype(jnp.int32), gs=gs, yd=yd, ys=ys, n_valid=n_valid.reshape(1),
                tile_e=tile_e.astype(jnp.int32), n_live=(ends[-1] // GRP).astype(jnp.int32).reshape(1),
                n_blk=n_blk, n_pairs=n_pairs, n_ffn=n_ffn)


def _moe_residual_norm(h, gain, shift, scale, gate, w_router, final_gain, wg, wu, wd, tf=1792):
    b, s, d = h.shape
    n_tok = b * s
    tpb = s // TMX
    nt = n_tok // TMX
    ne, _, dff = wg.shape
    vec = pl.BlockSpec((1, 1, d), lambda bi, i: (bi, 0, 0))
    low = jnp.asarray(np.tril(np.ones((TMX, TMX), np.float32), -1), BF16)
    u, info, cnt = pl.pallas_call(
        _router_kernel,
        grid=(b, tpb),
        in_specs=[pl.BlockSpec((1, TMX, d), lambda bi, i: (bi, i, 0)),
                  pl.BlockSpec((1, d), lambda bi, i: (0, 0)),
                  vec, vec,
                  pl.BlockSpec((d, LANE), lambda bi, i: (0, 0)),
                  pl.BlockSpec((TMX, TMX), lambda bi, i: (0, 0))],
        out_specs=[pl.BlockSpec((1, TMX, d), lambda bi, i: (bi, i, 0)),
                   pl.BlockSpec((1, TMX, LANE), lambda bi, i: (bi, i, 0)),
                   pl.BlockSpec((1, 8, LANE), lambda bi, i: (bi * tpb + i, 0, 0))],
        out_shape=[jax.ShapeDtypeStruct((b, s, d), BF16),
                   jax.ShapeDtypeStruct((b, s, LANE), F32),
                   jax.ShapeDtypeStruct((nt, 8, LANE), F32)],
        scratch_shapes=[pltpu.VMEM((8, LANE), F32)],
        compiler_params=_cparams("arbitrary", "arbitrary"),
        name="moe_router",
    )(h, gain.reshape(1, d), shift, scale, w_router, low)

    info = info.reshape(n_tok, LANE)
    plan = _moe_plan(info, cnt, n_tok)
    n_blk, n_pairs, n_ffn = plan["n_blk"], plan["n_pairs"], plan["n_ffn"]
    pad6 = jnp.zeros((nt, 6, TMX), jnp.int32)
    dr_rows = jnp.concatenate([plan["dr1"].reshape(nt, 1, TMX), plan["dr2"].reshape(nt, 1, TMX), pad6], axis=1)
    w_rows = jnp.concatenate([info[:, 4].reshape(nt, 1, TMX), info[:, 5].reshape(nt, 1, TMX),
                              pad6.astype(F32)], axis=1)
    dr_cols = jnp.pad(jnp.stack([plan["dr1"], plan["dr2"]], axis=1), ((0, 0), (0, LANE - 2)))

    xs, ws = pl.pallas_call(
        _moe_gather_kernel,
        grid_spec=pltpu.PrefetchScalarGridSpec(
            num_scalar_prefetch=2,
            grid=(n_blk,),
            in_specs=[pl.BlockSpec((n_tok, d), lambda k, gb, gs: (0, 0), pipeline_mode=pl.Buffered(1)),
                      pl.BlockSpec((nt, 8, TMX), lambda k, gb, gs: (0, 0, 0), pipeline_mode=pl.Buffered(1)),
                      pl.BlockSpec((nt, 8, TMX), lambda k, gb, gs: (0, 0, 0), pipeline_mode=pl.Buffered(1))],
            out_specs=[pl.BlockSpec((RBLK, d), lambda k, gb, gs: (k, 0)),
                       pl.BlockSpec((RBLK, LANE), lambda k, gb, gs: (k, 0))]),
        out_shape=[jax.ShapeDtypeStruct((n_blk * RBLK, d), BF16),
                   jax.ShapeDtypeStruct((n_blk * RBLK, LANE), F32)],
        compiler_params=_cparams("arbitrary"),
        name="moe_gather",
    )(plan["gb"], plan["gs"], u.reshape(n_tok, d), dr_rows, w_rows)

    def live_tile(i, te, nl):
        return jnp.maximum(jnp.minimum(i, nl[0] - 1), 0)

    rows_out = pl.pallas_call(
        _moe_ffn_kernel,
        grid_spec=pltpu.PrefetchScalarGridSpec(
            num_scalar_prefetch=2,
            grid=(n_ffn, dff // tf),
            in_specs=[pl.BlockSpec((GRP, d), lambda i, f, te, nl: (live_tile(i, te, nl), 0)),
                      pl.BlockSpec((GRP, LANE), lambda i, f, te, nl: (live_tile(i, te, nl), 0)),
                      pl.BlockSpec((1, d, tf), lambda i, f, te, nl: (te[i], 0, f)),
                      pl.BlockSpec((1, d, tf), lambda i, f, te, nl: (te[i], 0, f)),
                      pl.BlockSpec((1, tf, d), lambda i, f, te, nl: (te[i], f, 0))],
            out_specs=pl.BlockSpec((GRP, d), lambda i, f, te, nl: (i, 0)),
            scratch_shapes=[pltpu.VMEM((GRP, d), F32)]),
        out_shape=jax.ShapeDtypeStruct((n_blk * RBLK, d), BF16),
        compiler_params=_cparams("arbitrary", "arbitrary"),
        name="moe_experts",
    )(plan["tile_e"], plan["n_live"], xs, ws, wg, wu, wd)

    return pl.pallas_call(
        _moe_combine_kernel,
        grid_spec=pltpu.PrefetchScalarGridSpec(
            num_scalar_prefetch=3,
            grid=(n_pairs,),
            in_specs=[pl.BlockSpec((RBLK, d), lambda k, ys, yd, nv: (yd[k], 0)),
                      pl.BlockSpec((TMX, LANE), lambda k, ys, yd, nv: (ys[k], 0)),
                      pl.BlockSpec((1, TMX, d), lambda k, ys, yd, nv: (ys[k] // tpb, ys[k] % tpb, 0)),
                      pl.BlockSpec((1, 1, d), lambda k, ys, yd, nv: (ys[k] // tpb, 0, 0)),
                      pl.BlockSpec((1, d), lambda k, ys, yd, nv: (0, 0))],
            out_specs=pl.BlockSpec((1, TMX, d), lambda k, ys, yd, nv: (ys[k] // tpb, ys[k] % tpb, 0)),
            scratch_shapes=[pltpu.VMEM((TMX, d), F32)]),
        out_shape=jax.ShapeDtypeStruct((b, s, d), F32),
        compiler_params=_cparams("arbitrary"),
        name="moe_combine_norm",
    )(plan["ys"], plan["yd"], plan["n_valid"], rows_out, dr_cols, h, gate, final_gain.reshape(1, d))


def _pad_cols(w, n):
    return jnp.pad(w, ((0, 0), (0, n - w.shape[1])))


def _dsa_weights(w_in):
    q, k, v, iq, ik, iw = jnp.split(w_in, [1024, 1280, 1536, 2048, 2112], axis=1)
    w = jnp.concatenate([q, iq, k, v, _pad_cols(ik, LANE), _pad_cols(iw, LANE)], axis=1).astype(BF16)
    plan = [(0, 1024, True, Q_SCALE, 0, 0, "spread"),
            (1024, 512, True, 1.0, 0, 2048, "spread"),
            (1536, 256, True, 1.0, 0, 3072, "spread"),
            (1792, 256, False, 1.0, 0, 3584, "spread+ones"),
            (2048, 128, True, 1.0, 0, 4096, None),
            (2176, 128, False, IDX_HEADS ** -0.5 * IDX_DIM ** -0.5, 1, 0, None)]
    return w, plan, [(4224, BF16), (LANE, F32)]


def _nsa_q_weights(w_q):
    q = w_q[:, :N_HEADS * HEAD_DIM]
    gates = w_q[:, N_HEADS * HEAD_DIM:].reshape(-1, N_HEADS, N_NSA_BRANCH)
    gates = jnp.transpose(gates, (0, 2, 1)).reshape(-1, N_NSA_BRANCH * N_HEADS)
    w = jnp.concatenate([q, _pad_cols(gates, LANE)], axis=1).astype(BF16)
    plan = [(0, 1024, True, Q_SCALE, 0, 0, "spread"),
            (1024, 128, False, 1.0, 1, 0, "sigmoid")]
    return w, plan, [(2048, BF16), (LANE, F32)]


def _kv_weights(w_kv):
    kvw = w_kv.reshape(-1, 2 * N_NSA_BRANCH, N_KV_HEADS * HEAD_DIM)
    k_cmp, v_cmp, k_slc, v_slc, k_win, v_win = [kvw[:, j] for j in range(2 * N_NSA_BRANCH)]
    w = jnp.concatenate([k_slc, k_win, v_slc, v_win, k_cmp, v_cmp], axis=1).astype(BF16)
    plan = [(0, 256, True, 1.0, 0, 0, "spread+blockhot"),
            (256, 256, True, 1.0, 0, 512, "spread"),
            (512, 256, False, 1.0, 0, 1024, "spread+ones"),
            (768, 256, False, 1.0, 0, 1536, "spread+ones"),
            (1024, 256, True, 1.0, 1, 0, None),
            (1280, 256, False, 1.0, 1, 256, None)]
    return w, plan, [(2048, BF16), (512, F32)]


def _nsa_constants(s_len):
    n_cmp = (s_len - CMP_LEN) // CMP_STRIDE + 1
    n_slc = s_len // SLC_LEN
    cmp_start = np.arange(n_cmp) * CMP_STRIDE
    slc_start = np.arange(n_slc) * SLC_LEN
    ov = (np.minimum(cmp_start[:, None] + CMP_LEN, slc_start[None, :] + SLC_LEN)
          - np.maximum(cmp_start[:, None], slc_start[None, :]))
    agg = (np.clip(ov, 0, None) / CMP_LEN).astype(np.float32)
    agg_t = np.zeros((n_slc, s_len // CMP_STRIDE), np.float32)
    agg_t[:, :n_cmp] = agg.T
    return jnp.asarray(agg_t)


def kernel(x, c, positions, attn_gain, ffn_gain, w_ada, b_ada, a_w_in, a_w_out, b_w_q, b_w_out, kv_gain, w_kv_ada, b_kv_ada, w_kv, cmp_pe_k, cmp_w1_k, cmp_w2_k, cmp_pe_v, cmp_w1_v, cmp_w2_v, ffn_w_gate, ffn_w_up, ffn_w_down, moe_w_router, moe_w_gate, moe_w_up, moe_w_down, final_gain):
    b, s_len, d = x.shape
    assert s_len % TK == 0 and s_len // SLC_LEN <= HEAD_DIM and s_len >= WIN_SPAN and b <= 8

    cos_t, sin_t = _rope_tables(positions)
    c_pad = jnp.pad(c, ((0, 8 - b), (0, 0)))
    mod = _ada(c_pad, w_ada, b_ada)[:, :b]
    kv_mod = _ada(c_pad, w_kv_ada[None], b_kv_ada[None])[0, :b]

    def parts(m, n):
        return [p[:, None, :] for p in jnp.split(m, n, axis=-1)]

    a_shift, a_scale, a_gate, f_shift, f_scale, f_gate = parts(mod[0], 6)
    w, plan, outs = _dsa_weights(a_w_in[0])
    proj, iw = _norm_proj(x, attn_gain[0], a_shift, a_scale, cos_t, sin_t, w, plan, outs)
    bias = _dsa_bias(proj, iw, s_len)
    o = _dsa_attention(proj, bias)
    h = _outproj_residual(o, _out_rows(a_w_out[0]).astype(BF16), a_gate, x)
    h = _ffn_residual(h, ffn_gain[0], f_shift, f_scale, f_gate,
                      ffn_w_gate[0].astype(BF16), ffn_w_up[0].astype(BF16), ffn_w_down[0].astype(BF16))

    kv_shift, kv_scale = parts(kv_mod, 2)
    w, plan, outs = _kv_weights(w_kv)
    kv, kv_cmp = _norm_proj(h, kv_gain, kv_shift, kv_scale, cos_t, sin_t, w, plan, outs)
    nr = s_len // CMP_STRIDE
    xc = kv_cmp.reshape(b, s_len, 2, N_KV_HEADS, HEAD_DIM)
    xc = jnp.transpose(xc, (2, 0, 3, 1, 4)).reshape(2, b, N_KV_HEADS, nr, CMP_STRIDE * HEAD_DIM)
    pe = jnp.stack([cmp_pe_k, cmp_pe_v]).reshape(2, 2, 1, CMP_STRIDE * HEAD_DIM)
    w1 = jnp.stack([cmp_w1_k, cmp_w1_v]).reshape(2, 2, CMP_STRIDE * HEAD_DIM, CMP_HIDDEN).astype(BF16)
    w2 = jnp.stack([cmp_w2_k, cmp_w2_v]).astype(BF16)
    cmp = _compress(xc, pe, w1, w2)
    cmp = jnp.transpose(cmp, (0, 1, 3, 2, 4))
    cmp = jnp.pad(cmp, ((0, 0), (0, 0), (0, 0), (0, 0), (0, LANE - HEAD_DIM)))
    kc, vc = cmp.reshape(2, b, nr, N_KV_HEADS * LANE)

    a_shift, a_scale, a_gate, f_shift, f_scale, f_gate = parts(mod[1], 6)
    w, plan, outs = _nsa_q_weights(b_w_q[0])
    q, gates = _norm_proj(h, attn_gain[1], a_shift, a_scale, cos_t, sin_t, w, plan, outs)
    o = _nsa_attention(q, gates, kc, vc, kv, _nsa_constants(s_len))
    h = _outproj_residual(o, _out_rows(b_w_out[0]).astype(BF16), a_gate, h)
    return _moe_residual_norm(h, ffn_gain[1], f_shift, f_scale, f_gate, _pad_cols(moe_w_router[0], LANE),
                              final_gain, moe_w_gate[0].astype(BF16), moe_w_up[0].astype(BF16),
                              moe_w_down[0].astype(BF16))
```
